```python
import math
import jax, jax.numpy as jnp
from jax import lax
import numpy as np

D_MODEL = 1024
BATCH = 4
SEQ = 4096
DEPTH = 1
DEC_BATCH = 4
DEC_SEQ = 8192
PAST_LEN = 128

D_RNN = 1024
RNN_BLOCKS = 16
RNN_BLOCK_W = D_RNN // RNN_BLOCKS
RNN_CONV_W = 4
RNN_CONV_LEFT = 2
LRU_C = 8.0
N_HEADS = 8
Q_LORA = 384
KV_LORA = 256
QK_NOPE = 128
QK_ROPE = 64
V_HEAD = 128
ROPE_THETA = 10000.0
Q_BLOCK = 128
D_FF = 3072
FFN_CONV_W = 3
FFN_CONV_LEFT = 1
N_BRANCHES = 2
LN_EPS = 1e-5
RMS_EPS = 1e-6
ALPHA = (2.0 * DEPTH) ** 0.25
BETA = (8.0 * DEPTH) ** -0.25
D_IN = 2 * D_RNN + Q_LORA + KV_LORA + QK_ROPE + N_BRANCHES * D_MODEL
SPLITS = (D_RNN, 2 * D_RNN, 2 * D_RNN + Q_LORA, 2 * D_RNN + Q_LORA + KV_LORA,
          2 * D_RNN + Q_LORA + KV_LORA + QK_ROPE)

kernel_name = "hybrid_rglru_mla_convglu_encoder"


def layer_norm(x, g, b):
    xf = x.astype(jnp.float32)
    mu = jnp.mean(xf, axis=-1, keepdims=True)
    xc = xf - mu
    var = jnp.mean(xc * xc, axis=-1, keepdims=True)
    y = xc * lax.rsqrt(var + LN_EPS) * g.astype(jnp.float32) + b.astype(jnp.float32)
    return y.astype(x.dtype)


def rms_norm(x, g):
    xf = x.astype(jnp.float32)
    y = xf * lax.rsqrt(jnp.mean(xf * xf, axis=-1, keepdims=True) + RMS_EPS) * g.astype(jnp.float32)
    return y.astype(x.dtype)


def depthwise_conv(x, w, b, left):
    width = w.shape[0]
    S = x.shape[1]
    xp = jnp.pad(x, ((0, 0), (left, width - 1 - left), (0, 0)))
    out = b + w[0] * xp[:, 0:S]
    for k in range(1, width):
        out = out + w[k] * xp[:, k:k + S]
    return out


def rope_tables(S):
    pos = jnp.arange(S, dtype=jnp.float32)
    inv = ROPE_THETA ** (-jnp.arange(0, QK_ROPE, 2, dtype=jnp.float32) / QK_ROPE)
    ang = pos[:, None] * inv[None, :]
    return jnp.cos(ang), jnp.sin(ang)


def apply_rope(x, cos, sin):
    half = QK_ROPE // 2
    xf = x.astype(jnp.float32)
    x1, x2 = xf[..., :half], xf[..., half:]
    return jnp.concatenate([x1 * cos - x2 * sin, x2 * cos + x1 * sin], axis=-1).astype(x.dtype)


def lru_coeffs(u, w_a, b_a, w_x, b_x, lam):
    Bsz, S, _ = u.shape
    ub = u.reshape(Bsz, S, RNN_BLOCKS, RNN_BLOCK_W)
    r = jax.nn.sigmoid(jnp.einsum('bsni,nij->bsnj', ub, w_a).reshape(Bsz, S, D_RNN) + b_a)
    i = jax.nn.sigmoid(jnp.einsum('bsni,nij->bsnj', ub, w_x).reshape(Bsz, S, D_RNN) + b_x)
    log_a = -LRU_C * r.astype(jnp.float32) * jax.nn.softplus(-lam.astype(jnp.float32))
    a = jnp.exp(log_a)
    b = jnp.sqrt(-jnp.expm1(2.0 * log_a)) * (i * u).astype(jnp.float32)
    return a, b


def _combine(lhs, rhs):
    a1, b1 = lhs
    a2, b2 = rhs
    return a1 * a2, a2 * b1 + b2


def linear_scan(a, b):
    return lax.associative_scan(_combine, (a, b), axis=1)[1]


def mla_attention(q_nope, q_rope, k_nope, k_rope, v):
    Bsz, S, H, _ = q_nope.shape
    nq = S // Q_BLOCK
    scale = (QK_NOPE + QK_ROPE) ** -0.5

    def to_blocks(t):
        return t.reshape(Bsz, nq, Q_BLOCK, *t.shape[2:]).swapaxes(0, 1)

    def block(qs):
        qn, qr = qs
        s = (jnp.einsum('bqhd,bkhd->bhqk', qn, k_nope).astype(jnp.float32)
             + jnp.einsum('bqhr,bkr->bhqk', qr, k_rope).astype(jnp.float32))
        p = jax.nn.softmax(s * scale, axis=-1).astype(v.dtype)
        return jnp.einsum('bhqk,bkhv->bqhv', p, v)

    o = lax.map(block, (to_blocks(q_nope), to_blocks(q_rope)))
    return o.swapaxes(0, 1).reshape(Bsz, S, H * V_HEAD)


def token_mixer(h, w_in, rnn_conv_w, rnn_conv_b, lru_w_a, lru_b_a, lru_w_x, lru_b_x, lru_lambda,
                w_o_rnn, q_norm_g, w_uq, kv_norm_g, w_ukv, w_o_attn, b_gate, w_out, cos, sin):
    Bsz, S, _ = h.shape
    proj = h @ w_in
    x_rnn, g_rnn, q_lat, kv_lat, k_r, gate_logits = jnp.split(proj, SPLITS, axis=-1)

    u = depthwise_conv(x_rnn, rnn_conv_w, rnn_conv_b, RNN_CONV_LEFT)
    a_f, b_f = lru_coeffs(u, lru_w_a[0], lru_b_a[0], lru_w_x[0], lru_b_x[0], lru_lambda[0])
    a_b, b_b = lru_coeffs(u, lru_w_a[1], lru_b_a[1], lru_w_x[1], lru_b_x[1], lru_lambda[1])
    h_f = linear_scan(a_f, b_f)
    h_b = jnp.flip(linear_scan(jnp.flip(a_b, 1), jnp.flip(b_b, 1)), 1)
    y_rnn = ((h_f + h_b).astype(h.dtype) * jax.nn.gelu(g_rnn)) @ w_o_rnn

    q = (rms_norm(q_lat, q_norm_g) @ w_uq).reshape(Bsz, S, N_HEADS, QK_NOPE + QK_ROPE)
    q_nope = q[..., :QK_NOPE]
    q_rope = apply_rope(q[..., QK_NOPE:], cos[:, None, :], sin[:, None, :])
    kv = (rms_norm(kv_lat, kv_norm_g) @ w_ukv).reshape(Bsz, S, N_HEADS, QK_NOPE + V_HEAD)
    k_nope, v = kv[..., :QK_NOPE], kv[..., QK_NOPE:]
    k_rope = apply_rope(k_r, cos, sin)
    y_attn = mla_attention(q_nope, q_rope, k_nope, k_rope, v) @ w_o_attn

    gates = jax.nn.sigmoid(gate_logits + b_gate)
    merged = gates[..., :D_MODEL] * y_rnn + gates[..., D_MODEL:] * y_attn
    return merged @ w_out


def conv_glu(h, ffn_w_in, ffn_conv_w, ffn_conv_b, ffn_w_out):
    hu = h @ ffn_w_in
    gate, up = hu[..., :D_FF], hu[..., D_FF:]
    gate = depthwise_conv(gate, ffn_conv_w, ffn_conv_b, FFN_CONV_LEFT)
    return (jax.nn.gelu(gate) * up) @ ffn_w_out


def encoder_trunk(x, ln_in_g, ln_in_b, w_in, rnn_conv_w, rnn_conv_b, lru_w_a, lru_b_a, lru_w_x,
                  lru_b_x, lru_lambda, w_o_rnn, q_norm_g, w_uq, kv_norm_g, w_ukv, w_o_attn, b_gate,
                  w_out, ln1_g, ln1_b, ffn_w_in, ffn_conv_w, ffn_conv_b, ffn_w_out, ln2_g, ln2_b):
    cos, sin = rope_tables(x.shape[1])
    h = layer_norm(x, ln_in_g, ln_in_b)
    for l in range(DEPTH):
        mix = token_mixer(h, w_in[l], rnn_conv_w[l], rnn_conv_b[l], lru_w_a[l], lru_b_a[l],
                          lru_w_x[l], lru_b_x[l], lru_lambda[l], w_o_rnn[l], q_norm_g[l], w_uq[l],
                          kv_norm_g[l], w_ukv[l], w_o_attn[l], b_gate[l], w_out[l], cos, sin)
        h = layer_norm(ALPHA * h + mix, ln1_g[l], ln1_b[l])
        ffn = conv_glu(h, ffn_w_in[l], ffn_conv_w[l], ffn_conv_b[l], ffn_w_out[l])
        h = layer_norm(ALPHA * h + ffn, ln2_g[l], ln2_b[l])
    return h


def setup_inputs(seed: int = 0) -> dict:
    key = jax.random.key(seed)
    ks = jax.random.split(key, 32)
    L = DEPTH
    f32 = jnp.float32

    def nrm(k, shape, scale):
        return jax.random.normal(k, shape, f32) * scale

    def gain(k, shape):
        return 1.0 + 0.01 * jax.random.normal(k, shape, f32)

    a0 = jax.random.uniform(ks[9], (L, 2, D_RNN), f32, 0.9, 0.999)
    sig = a0 ** (1.0 / LRU_C)
    lru_lambda = jnp.log(sig) - jnp.log1p(-sig)
    return {
        'x_prompt': nrm(ks[0], (BATCH, SEQ, D_MODEL), 1.0),
        'x_sample': nrm(ks[1], (DEC_BATCH, DEC_SEQ, D_MODEL), 1.0),
        'ln_in_g': gain(ks[2], (D_MODEL,)),
        'ln_in_b': nrm(ks[3], (D_MODEL,), 0.01),
        'w_in': nrm(ks[4], (L, D_MODEL, D_IN), D_MODEL ** -0.5),
        'rnn_conv_w': nrm(ks[5], (L, RNN_CONV_W, D_RNN), RNN_CONV_W ** -0.5),
        'rnn_conv_b': nrm(ks[6], (L, D_RNN), 0.01),
        'lru_w_a': nrm(ks[7], (L, 2, RNN_BLOCKS, RNN_BLOCK_W, RNN_BLOCK_W), RNN_BLOCK_W ** -0.5),
        'lru_b_a': nrm(ks[8], (L, 2, D_RNN), 0.01),
        'lru_w_x': nrm(ks[10], (L, 2, RNN_BLOCKS, RNN_BLOCK_W, RNN_BLOCK_W), RNN_BLOCK_W ** -0.5),
        'lru_b_x': nrm(ks[11], (L, 2, D_RNN), 0.01),
        'lru_lambda': lru_lambda,
        'w_o_rnn': nrm(ks[12], (L, D_RNN, D_MODEL), BETA * D_RNN ** -0.5),
        'q_norm_g': gain(ks[13], (L, Q_LORA)),
        'w_uq': nrm(ks[14], (L, Q_LORA, N_HEADS * (QK_NOPE + QK_ROPE)), Q_LORA ** -0.5),
        'kv_norm_g': gain(ks[15], (L, KV_LORA)),
        'w_ukv': nrm(ks[16], (L, KV_LORA, N_HEADS * (QK_NOPE + V_HEAD)), KV_LORA ** -0.5),
        'w_o_attn': nrm(ks[17], (L, N_HEADS * V_HEAD, D_MODEL), BETA * (N_HEADS * V_HEAD) ** -0.5),
        'b_gate': nrm(ks[18], (L, N_BRANCHES * D_MODEL), 0.01),
        'w_out': nrm(ks[19], (L, D_MODEL, D_MODEL), BETA * D_MODEL ** -0.5),
        'ln1_g': gain(ks[20], (L, D_MODEL)),
        'ln1_b': nrm(ks[21], (L, D_MODEL), 0.01),
        'ffn_w_in': nrm(ks[22], (L, D_MODEL, 2 * D_FF), D_MODEL ** -0.5),
        'ffn_conv_w': nrm(ks[23], (L, FFN_CONV_W, D_FF), FFN_CONV_W ** -0.5),
        'ffn_conv_b': nrm(ks[24], (L, D_FF), 0.01),
        'ffn_w_out': nrm(ks[25], (L, D_FF, D_MODEL), BETA * D_FF ** -0.5),
        'ln2_g': gain(ks[26], (L, D_MODEL)),
        'ln2_b': nrm(ks[27], (L, D_MODEL), 0.01),
    }


def reference(x_prompt, x_sample, ln_in_g, ln_in_b, w_in, rnn_conv_w, rnn_conv_b, lru_w_a, lru_b_a,
              lru_w_x, lru_b_x, lru_lambda, w_o_rnn, q_norm_g, w_uq, kv_norm_g, w_ukv, w_o_attn,
              b_gate, w_out, ln1_g, ln1_b, ffn_w_in, ffn_conv_w, ffn_conv_b, ffn_w_out, ln2_g, ln2_b):
    y_prompt = encoder_trunk(x_prompt, ln_in_g, ln_in_b, w_in, rnn_conv_w, rnn_conv_b, lru_w_a,
                             lru_b_a, lru_w_x, lru_b_x, lru_lambda, w_o_rnn, q_norm_g, w_uq,
                             kv_norm_g, w_ukv, w_o_attn, b_gate, w_out, ln1_g, ln1_b, ffn_w_in,
                             ffn_conv_w, ffn_conv_b, ffn_w_out, ln2_g, ln2_b)
    y_sample = encoder_trunk(x_sample, ln_in_g, ln_in_b, w_in, rnn_conv_w, rnn_conv_b, lru_w_a,
                             lru_b_a, lru_w_x, lru_b_x, lru_lambda, w_o_rnn, q_norm_g, w_uq,
                             kv_norm_g, w_ukv, w_o_attn, b_gate, w_out, ln1_g, ln1_b, ffn_w_in,
                             ffn_conv_w, ffn_conv_b, ffn_w_out, ln2_g, ln2_b)
    return (y_prompt, y_sample)
```

```python
import functools

import jax
import jax.numpy as jnp
from jax import lax
from jax.experimental import pallas as pl
from jax.experimental.pallas import tpu as pltpu

D_MODEL = 1024
D_RNN = 1024
RNN_BLOCKS = 16
RNN_BLOCK_W = D_RNN // RNN_BLOCKS
LRU_C = 8.0
N_HEADS = 8
Q_LORA = 384
KV_LORA = 256
QK_NOPE = 128
QK_ROPE = 64
QK_DIM = QK_NOPE + QK_ROPE
V_HEAD = 128
ROPE_THETA = 10000.0
D_FF = 3072
LN_EPS = 1e-5
RMS_EPS = 1e-6
DEPTH = 1
ALPHA = (2.0 * DEPTH) ** 0.25

SUBLANES = 8
LANES = 128
BF16_ROWS = 16
VMEM_LIMIT = 56 * 1024 * 1024

TM_PROJ = 512
TT_RNN = 512
RNN_GROUP = 256
TQ_ATTN = 512
TK_ATTN = 512
TM_MIX = 512
TM_FFN = 512
FF_CHUNK = 512

BF16 = jnp.bfloat16
F32 = jnp.float32


def _layer_norm(x, g, b):
    mu = jnp.mean(x, axis=-1, keepdims=True)
    xc = x - mu
    var = jnp.mean(xc * xc, axis=-1, keepdims=True)
    return xc * lax.rsqrt(var + LN_EPS) * g + b


def _rms_norm(x, g):
    return x * lax.rsqrt(jnp.mean(x * x, axis=-1, keepdims=True) + RMS_EPS) * g


def _dot(a, b):
    return jnp.dot(a, b, preferred_element_type=F32)


def _const_spec(shape):
    nd = len(shape)
    return pl.BlockSpec(shape, lambda *_: (0,) * nd, pipeline_mode=pl.Buffered(1))


def _proj_kernel(x_ref, lng_ref, lnb_ref, w_rnn_ref, w_gate_ref, bgate_ref, w_q_ref, qg_ref, w_uq_ref,
                 w_kv_ref, kvg_ref, w_ukv_ref, w_kr_ref, cos_ref, sin_ref,
                 xr_ref, gg_ref, gate_ref, q_ref, k_ref, v_ref):
    h = _layer_norm(x_ref[...], lng_ref[...], lnb_ref[...]).astype(BF16)

    rnn = _dot(h, w_rnn_ref[...])
    xr_ref[...] = rnn[:, :D_RNN].astype(BF16)
    gg_ref[...] = jax.nn.gelu(rnn[:, D_RNN:], approximate=True).astype(BF16)

    gate_ref[...] = jax.nn.sigmoid(_dot(h, w_gate_ref[...]) + bgate_ref[...]).astype(BF16)

    cos2 = cos_ref[...]
    sin2 = sin_ref[...]
    cos_q = jnp.concatenate([cos2] * (N_HEADS * QK_ROPE // LANES), axis=1)
    sin_q = jnp.concatenate([sin2] * (N_HEADS * QK_ROPE // LANES), axis=1)

    qn = _rms_norm(_dot(h, w_q_ref[...]), qg_ref[...]).astype(BF16)
    qf = _dot(qn, w_uq_ref[...])
    n_nope = N_HEADS * QK_NOPE
    n_rope = N_HEADS * QK_ROPE
    scale = QK_DIM ** -0.5
    q_nope = qf[:, :n_nope] * scale
    q_rope = (qf[:, n_nope:n_nope + n_rope] * cos_q + qf[:, n_nope + n_rope:] * sin_q) * scale

    kvn = _rms_norm(_dot(h, w_kv_ref[...]), kvg_ref[...]).astype(BF16)
    kv = _dot(kvn, w_ukv_ref[...])
    kr = _dot(h, w_kr_ref[...])
    k_rope = (kr[:, :QK_ROPE] * cos2[:, :QK_ROPE] + kr[:, QK_ROPE:] * sin2[:, :QK_ROPE]).astype(BF16)

    for hd in range(N_HEADS):
        q_ref[hd] = jnp.concatenate(
            [q_nope[:, hd * QK_NOPE:(hd + 1) * QK_NOPE], q_rope[:, hd * QK_ROPE:(hd + 1) * QK_ROPE]],
            axis=1).astype(BF16)
        base = hd * (QK_NOPE + V_HEAD)
        k_ref[hd] = jnp.concatenate([kv[:, base:base + QK_NOPE].astype(BF16), k_rope], axis=1)
        v_ref[hd] = kv[:, base + QK_NOPE:base + QK_NOPE + V_HEAD].astype(BF16)


def _proj(x2, seq, p):
    n = x2.shape[0]
    tm = TM_PROJ
    nt_seq = seq // tm
    row = lambda i: (i, 0)
    pos = lambda i: (i % nt_seq, 0)
    head_row = lambda i: (0, i, 0)
    consts = [p['ln_in_g'], p['ln_in_b'], p['w_rnn'], p['w_gate'], p['b_gate'], p['w_q'], p['q_norm_g'],
              p['w_uq'], p['w_kv'], p['kv_norm_g'], p['w_ukv'], p['w_kr']]
    in_specs = ([pl.BlockSpec((tm, D_MODEL), row)] + [_const_spec(c.shape) for c in consts]
                + [pl.BlockSpec((tm, LANES), pos), pl.BlockSpec((tm, LANES), pos)])
    out_shape = [
        jax.ShapeDtypeStruct((n, D_RNN), BF16),
        jax.ShapeDtypeStruct((n, D_RNN), BF16),
        jax.ShapeDtypeStruct((n, 2 * D_MODEL), BF16),
        jax.ShapeDtypeStruct((N_HEADS, n, QK_DIM), BF16),
        jax.ShapeDtypeStruct((N_HEADS, n, QK_DIM), BF16),
        jax.ShapeDtypeStruct((N_HEADS, n, V_HEAD), BF16),
    ]
    out_specs = [
        pl.BlockSpec((tm, D_RNN), row), pl.BlockSpec((tm, D_RNN), row), pl.BlockSpec((tm, 2 * D_MODEL), row),
        pl.BlockSpec((N_HEADS, tm, QK_DIM), head_row), pl.BlockSpec((N_HEADS, tm, QK_DIM), head_row),
        pl.BlockSpec((N_HEADS, tm, V_HEAD), head_row),
    ]
    return pl.pallas_call(
        _proj_kernel, grid=(n // tm,), in_specs=in_specs, out_specs=out_specs, out_shape=out_shape,
        compiler_params=pltpu.CompilerParams(dimension_semantics=("parallel",), vmem_limit_bytes=VMEM_LIMIT),
        name="proj",
    )(x2, *consts, p['cos2'], p['sin2'])


def _rnn_kernel(xf_ref, xfp_ref, xfn_ref, xb_ref, xbp_ref, xbn_ref, cw_ref, cb_ref, wg_ref, ba_ref, bx_ref,
                lam_ref, hf_ref, hb_ref, xe_ref, a_ref, b_ref, carry_ref, *, n_tiles):
    j = pl.program_id(1)
    tt = TT_RNN
    halo = BF16_ROWS

    @pl.when(j == 0)
    def _():
        carry_ref[...] = jnp.zeros_like(carry_ref)

    sub = lax.broadcasted_iota(jnp.int32, (SUBLANES, D_RNN), 0)

    def one_direction(d, x_ref, xp_ref, xn_ref, tile, out_ref):
        prev = jnp.where(tile > 0, xp_ref[0].astype(F32), 0.0)
        nxt = jnp.where(tile < n_tiles - 1, xn_ref[0].astype(F32), 0.0)
        xe_ref[0:halo, :] = prev
        xe_ref[halo:halo + tt, :] = x_ref[0].astype(F32)
        xe_ref[halo + tt:2 * halo + tt, :] = nxt
        u = cb_ref[...] + cw_ref[0:1, :] * xe_ref[halo - 2:halo - 2 + tt, :]
        for k in range(1, 4):
            u = u + cw_ref[k:k + 1, :] * xe_ref[halo - 2 + k:halo - 2 + k + tt, :]

        sp = jax.nn.softplus(-lam_ref[d:d + 1, :])
        for g in range(D_RNN // RNN_GROUP):
            cs = slice(g * RNN_GROUP, (g + 1) * RNN_GROUP)
            ug = u[:, cs]
            z = _dot(ug.astype(BF16), wg_ref[d, g])
            r = jax.nn.sigmoid(z[:, :RNN_GROUP] + ba_ref[d:d + 1, cs])
            i = jax.nn.sigmoid(z[:, RNN_GROUP:] + bx_ref[d:d + 1, cs])
            log_a = -LRU_C * r * sp[:, cs]
            a = jnp.exp(log_a)
            a_ref[:, cs] = a
            b_ref[:, cs] = jnp.sqrt(1.0 - a * a) * (i * ug)

        n_groups = tt // SUBLANES
        reverse = d == 1

        def body(gi, carry):
            grp = (n_groups - 1 - gi) if reverse else gi
            r0 = pl.multiple_of(grp * SUBLANES, SUBLANES)
            av = a_ref[pl.ds(r0, SUBLANES), :]
            bv = b_ref[pl.ds(r0, SUBLANES), :]
            for s in (1, 2, 4):
                if reverse:
                    m = sub < SUBLANES - s
                    sh = SUBLANES - s
                else:
                    m = sub >= s
                    sh = s
                ar = jnp.where(m, pltpu.roll(av, sh, 0), 1.0)
                br = jnp.where(m, pltpu.roll(bv, sh, 0), 0.0)
                bv = av * br + bv
                av = av * ar
            hv = av * carry + bv
            b_ref[pl.ds(r0, SUBLANES), :] = hv
            edge = hv[0:1, :] if reverse else hv[SUBLANES - 1:SUBLANES, :]
            return jnp.broadcast_to(edge, (SUBLANES, D_RNN))

        carry_ref[d] = lax.fori_loop(0, n_groups, body, carry_ref[d], unroll=2)
        out_ref[0] = b_ref[...].astype(BF16)

    one_direction(0, xf_ref, xfp_ref, xfn_ref, j, hf_ref)
    one_direction(1, xb_ref, xbp_ref, xbn_ref, n_tiles - 1 - j, hb_ref)


def _rnn(xr3, p):
    bsz, seq, _ = xr3.shape
    tt = TT_RNN
    nt = seq // tt
    hpt = tt // BF16_ROWS
    n_halo = seq // BF16_ROWS
    fwd = lambda b, j: (b, j, 0)
    fwd_p = lambda b, j: (b, jnp.maximum(j * hpt - 1, 0), 0)
    fwd_n = lambda b, j: (b, jnp.minimum((j + 1) * hpt, n_halo - 1), 0)
    bwd = lambda b, j: (b, nt - 1 - j, 0)
    bwd_p = lambda b, j: (b, jnp.maximum((nt - 1 - j) * hpt - 1, 0), 0)
    bwd_n = lambda b, j: (b, jnp.minimum((nt - j) * hpt, n_halo - 1), 0)
    consts = [p['rnn_conv_w'], p['rnn_conv_b'], p['w_lru'], p['lru_b_a'], p['lru_b_x'], p['lru_lambda']]
    main = (1, tt, D_RNN)
    hal = (1, BF16_ROWS, D_RNN)
    in_specs = [pl.BlockSpec(main, fwd), pl.BlockSpec(hal, fwd_p), pl.BlockSpec(hal, fwd_n),
                pl.BlockSpec(main, bwd), pl.BlockSpec(hal, bwd_p), pl.BlockSpec(hal, bwd_n)]
    in_specs += [_const_spec(c.shape) for c in consts]
    out_shape = [jax.ShapeDtypeStruct((bsz, seq, D_RNN), BF16)] * 2
    out_specs = [pl.BlockSpec(main, fwd), pl.BlockSpec(main, bwd)]
    scratch = [pltpu.VMEM((tt + 2 * BF16_ROWS, D_RNN), F32), pltpu.VMEM((tt, D_RNN), F32),
               pltpu.VMEM((tt, D_RNN), F32), pltpu.VMEM((2, SUBLANES, D_RNN), F32)]
    return pl.pallas_call(
        functools.partial(_rnn_kernel, n_tiles=nt), grid=(bsz, nt), in_specs=in_specs, out_specs=out_specs,
        out_shape=out_shape, scratch_shapes=scratch,
        compiler_params=pltpu.CompilerParams(dimension_semantics=("parallel", "arbitrary"),
                                             vmem_limit_bytes=VMEM_LIMIT),
        name="rnn",
    )(xr3, xr3, xr3, xr3, xr3, xr3, *consts)


def _attn_kernel(q_ref, k_ref, v_ref, o_ref, m_ref, l_ref, acc_ref, *, n_kv):
    tk = TK_ATTN
    q = q_ref[0]
    m_ref[...] = jnp.full_like(m_ref, -jnp.inf)
    l_ref[...] = jnp.zeros_like(l_ref)
    acc_ref[...] = jnp.zeros_like(acc_ref)

    def body(i, _):
        r0 = pl.multiple_of(i * tk, tk)
        k = k_ref[0, pl.ds(r0, tk), :]
        v = v_ref[0, pl.ds(r0, tk), :]
        s = lax.dot_general(q, k, (((1,), (1,)), ((), ())), preferred_element_type=F32)
        m_prev = m_ref[...]
        m_new = jnp.maximum(m_prev, jnp.max(s, axis=-1, keepdims=True))
        alpha = jnp.exp(m_prev - m_new)
        pr = jnp.exp(s - m_new)
        l_ref[...] = alpha * l_ref[...] + jnp.sum(pr, axis=-1, keepdims=True)
        acc_ref[...] = alpha * acc_ref[...] + _dot(pr.astype(BF16), v)
        m_ref[...] = m_new
        return 0

    lax.fori_loop(0, n_kv, body, 0)
    o_ref[...] = (acc_ref[...] / l_ref[...]).astype(BF16)


def _attn(q, k, v, bsz, seq):
    n = bsz * seq
    tq = TQ_ATTN
    nq = seq // tq
    in_specs = [pl.BlockSpec((1, tq, QK_DIM), lambda b, h, i: (h, b * nq + i, 0)),
                pl.BlockSpec((1, seq, QK_DIM), lambda b, h, i: (h, b, 0)),
                pl.BlockSpec((1, seq, V_HEAD), lambda b, h, i: (h, b, 0))]
    out_specs = pl.BlockSpec((tq, V_HEAD), lambda b, h, i: (b * nq + i, h))
    scratch = [pltpu.VMEM((tq, 1), F32), pltpu.VMEM((tq, 1), F32), pltpu.VMEM((tq, V_HEAD), F32)]
    return pl.pallas_call(
        functools.partial(_attn_kernel, n_kv=seq // TK_ATTN), grid=(bsz, N_HEADS, nq), in_specs=in_specs,
        out_specs=out_specs, out_shape=jax.ShapeDtypeStruct((n, N_HEADS * V_HEAD), BF16),
        scratch_shapes=scratch,
        compiler_params=pltpu.CompilerParams(dimension_semantics=("parallel", "parallel", "arbitrary"),
                                             vmem_limit_bytes=VMEM_LIMIT),
        name="attn",
    )(q, k, v)


def _mix_kernel(x_ref, hf_ref, hb_ref, gg_ref, o_ref, gate_ref, lng_ref, lnb_ref, w_or_ref, w_oa_ref, w_out_ref,
                l1g_ref, l1b_ref, h1_ref):
    h = _layer_norm(x_ref[...], lng_ref[...], lnb_ref[...])
    rnn_in = (hf_ref[...].astype(F32) + hb_ref[...].astype(F32)) * gg_ref[...].astype(F32)
    y_rnn = _dot(rnn_in.astype(BF16), w_or_ref[...])
    y_attn = _dot(o_ref[...], w_oa_ref[...])
    gates = gate_ref[...].astype(F32)
    merged = gates[:, :D_MODEL] * y_rnn + gates[:, D_MODEL:] * y_attn
    mix = _dot(merged.astype(BF16), w_out_ref[...])
    h1_ref[...] = _layer_norm(ALPHA * h + mix, l1g_ref[...], l1b_ref[...])


def _mix(x2, hf, hb, gg, o, gate, p):
    n = x2.shape[0]
    tm = TM_MIX
    row = lambda i: (i, 0)
    acts = [x2, hf, hb, gg, o, gate]
    consts = [p['ln_in_g'], p['ln_in_b'], p['w_o_rnn'], p['w_o_attn'], p['w_out'], p['ln1_g'], p['ln1_b']]
    in_specs = [pl.BlockSpec((tm, a.shape[1]), row) for a in acts] + [_const_spec(c.shape) for c in consts]
    return pl.pallas_call(
        _mix_kernel, grid=(n // tm,), in_specs=in_specs, out_specs=pl.BlockSpec((tm, D_MODEL), row),
        out_shape=jax.ShapeDtypeStruct((n, D_MODEL), F32),
        compiler_params=pltpu.CompilerParams(dimension_semantics=("parallel",), vmem_limit_bytes=VMEM_LIMIT),
        name="mix",
    )(*acts, *consts)


def _ffn_kernel(h_ref, hp_ref, hn_ref, wg_ref, wu_ref, cw_ref, cb_ref, wo_ref, l2g_ref, l2b_ref, out_ref,
                ext_ref, g_ref, acc_ref, *, n_tiles):
    j = pl.program_id(1)
    tm = TM_FFN
    halo = SUBLANES
    hm = h_ref[0]
    ext_ref[0:halo, :] = jnp.where(j > 0, hp_ref[0], 0.0)
    ext_ref[halo:halo + tm, :] = hm
    ext_ref[halo + tm:2 * halo + tm, :] = jnp.where(j < n_tiles - 1, hn_ref[0], 0.0)
    ext = ext_ref[...].astype(BF16)
    hmb = hm.astype(BF16)
    acc_ref[...] = jnp.zeros_like(acc_ref)

    def body(c, _):
        g_ref[...] = _dot(ext, wg_ref[c])
        cw = cw_ref[c]
        gc = cb_ref[c] + cw[0:1, :] * g_ref[halo - 1:halo - 1 + tm, :]
        gc = gc + cw[1:2, :] * g_ref[halo:halo + tm, :]
        gc = gc + cw[2:3, :] * g_ref[halo + 1:halo + 1 + tm, :]
        up = _dot(hmb, wu_ref[c])
        act = (jax.nn.gelu(gc, approximate=True) * up).astype(BF16)
        acc_ref[...] += _dot(act, wo_ref[c])
        return 0

    lax.fori_loop(0, D_FF // FF_CHUNK, body, 0)
    out_ref[0] = _layer_norm(ALPHA * hm + acc_ref[...], l2g_ref[...], l2b_ref[...])


def _ffn(h1, p):
    bsz, seq, _ = h1.shape
    tm = TM_FFN
    nt = seq // tm
    hpt = tm // SUBLANES
    n_halo = seq // SUBLANES
    main = (1, tm, D_MODEL)
    hal = (1, SUBLANES, D_MODEL)
    cur = lambda b, j: (b, j, 0)
    prv = lambda b, j: (b, jnp.maximum(j * hpt - 1, 0), 0)
    nxt = lambda b, j: (b, jnp.minimum((j + 1) * hpt, n_halo - 1), 0)
    consts = [p['ffn_w_gate'], p['ffn_w_up'], p['ffn_conv_w'], p['ffn_conv_b'], p['ffn_w_out'], p['ln2_g'],
              p['ln2_b']]
    in_specs = [pl.BlockSpec(main, cur), pl.BlockSpec(hal, prv), pl.BlockSpec(hal, nxt)]
    in_specs += [_const_spec(c.shape) for c in consts]
    scratch = [pltpu.VMEM((tm + 2 * SUBLANES, D_MODEL), F32), pltpu.VMEM((tm + 2 * SUBLANES, FF_CHUNK), F32),
               pltpu.VMEM((tm, D_MODEL), F32)]
    return pl.pallas_call(
        functools.partial(_ffn_kernel, n_tiles=nt), grid=(bsz, nt), in_specs=in_specs,
        out_specs=pl.BlockSpec(main, cur), out_shape=jax.ShapeDtypeStruct((bsz, seq, D_MODEL), F32),
        scratch_shapes=scratch,
        compiler_params=pltpu.CompilerParams(dimension_semantics=("parallel", "parallel"),
                                             vmem_limit_bytes=VMEM_LIMIT),
        name="ffn",
    )(h1, h1, h1, *consts)


def _rot_cols(w):
    half = QK_ROPE // 2
    return jnp.concatenate([-w[..., half:], w[..., :half]], axis=-1)


def _block_diag_groups(w):
    per = RNN_GROUP // RNN_BLOCK_W
    w4 = w.reshape(D_RNN // RNN_GROUP, per, RNN_BLOCK_W, RNN_BLOCK_W)
    eye = jnp.eye(per, dtype=w.dtype)
    return jnp.einsum('gkij,kl->gkilj', w4, eye).reshape(D_RNN // RNN_GROUP, RNN_GROUP, RNN_GROUP)


def _prepare_params(ln_in_g, ln_in_b, w_in, rnn_conv_w, rnn_conv_b, lru_w_a, lru_b_a, lru_w_x, lru_b_x, lru_lambda,
                    w_o_rnn, q_norm_g, w_uq, kv_norm_g, w_ukv, w_o_attn, b_gate, w_out, ln1_g, ln1_b, ffn_w_in,
                    ffn_conv_w, ffn_conv_b, ffn_w_out, ln2_g, ln2_b):
    l = 0
    row = lambda v: v.reshape(1, -1).astype(F32)
    w = w_in[l]
    c0, c1, c2, c3 = 2 * D_RNN, 2 * D_RNN + Q_LORA, 2 * D_RNN + Q_LORA + KV_LORA, 2 * D_RNN + Q_LORA + KV_LORA + QK_ROPE
    w_kr = w[:, c2:c3]
    uq = w_uq[l].reshape(Q_LORA, N_HEADS, QK_DIM)
    uq_nope = uq[:, :, :QK_NOPE].reshape(Q_LORA, N_HEADS * QK_NOPE)
    uq_rope = uq[:, :, QK_NOPE:]
    n_chunks = D_FF // FF_CHUNK
    fw = ffn_w_in[l]
    chunk_cols = lambda m: m.reshape(m.shape[0], n_chunks, FF_CHUNK).transpose(1, 0, 2)
    return {
        'ln_in_g': row(ln_in_g), 'ln_in_b': row(ln_in_b),
        'w_rnn': w[:, :c0].astype(BF16),
        'w_q': w[:, c0:c1].astype(BF16),
        'w_kv': w[:, c1:c2].astype(BF16),
        'w_kr': jnp.concatenate([w_kr, _rot_cols(w_kr)], axis=1).astype(BF16),
        'w_gate': w[:, c3:].astype(BF16),
        'b_gate': row(b_gate[l]),
        'q_norm_g': row(q_norm_g[l]), 'kv_norm_g': row(kv_norm_g[l]),
        'w_uq': jnp.concatenate([uq_nope, uq_rope.reshape(Q_LORA, -1), _rot_cols(uq_rope).reshape(Q_LORA, -1)],
                                axis=1).astype(BF16),
        'w_ukv': w_ukv[l].astype(BF16),
        'rnn_conv_w': rnn_conv_w[l].astype(F32), 'rnn_conv_b': row(rnn_conv_b[l]),
        'w_lru': jnp.stack([jnp.concatenate([_block_diag_groups(lru_w_a[l, d]), _block_diag_groups(lru_w_x[l, d])],
                                            axis=2) for d in range(2)]).astype(BF16),
        'lru_b_a': lru_b_a[l].astype(F32), 'lru_b_x': lru_b_x[l].astype(F32), 'lru_lambda': lru_lambda[l].astype(F32),
        'w_o_rnn': w_o_rnn[l].astype(BF16), 'w_o_attn': w_o_attn[l].astype(BF16), 'w_out': w_out[l].astype(BF16),
        'ln1_g': row(ln1_g[l]), 'ln1_b': row(ln1_b[l]),
        'ffn_w_gate': chunk_cols(fw[:, :D_FF]).astype(BF16),
        'ffn_w_up': chunk_cols(fw[:, D_FF:]).astype(BF16),
        'ffn_conv_w': chunk_cols(ffn_conv_w[l]).astype(F32),
        'ffn_conv_b': ffn_conv_b[l].reshape(n_chunks, 1, FF_CHUNK).astype(F32),
        'ffn_w_out': ffn_w_out[l].reshape(n_chunks, FF_CHUNK, D_MODEL).astype(BF16),
        'ln2_g': row(ln2_g[l]), 'ln2_b': row(ln2_b[l]),
    }


def _rope_tables(seq):
    pos = jnp.arange(seq, dtype=F32)
    inv = ROPE_THETA ** (-jnp.arange(0, QK_ROPE, 2, dtype=F32) / QK_ROPE)
    ang = pos[:, None] * inv[None, :]
    reps = LANES // (QK_ROPE // 2)
    return jnp.tile(jnp.cos(ang), (1, reps)), jnp.tile(jnp.sin(ang), (1, reps))


def _trunk(x, p):
    bsz, seq, _ = x.shape
    n = bsz * seq
    x2 = x.reshape(n, D_MODEL)
    cos2, sin2 = _rope_tables(seq)
    pp = dict(p, cos2=cos2, sin2=sin2)
    xr, gg, gate, q, k, v = _proj(x2, seq, pp)
    hf, hb = _rnn(xr.reshape(bsz, seq, D_RNN), p)
    o = _attn(q, k, v, bsz, seq)
    h1 = _mix(x2, hf.reshape(n, D_RNN), hb.reshape(n, D_RNN), gg, o, gate, p)
    return _ffn(h1.reshape(bsz, seq, D_MODEL), p)


def kernel(x_prompt, x_sample, ln_in_g, ln_in_b, w_in, rnn_conv_w, rnn_conv_b, lru_w_a, lru_b_a, lru_w_x, lru_b_x,
           lru_lambda, w_o_rnn, q_norm_g, w_uq, kv_norm_g, w_ukv, w_o_attn, b_gate, w_out, ln1_g, ln1_b, ffn_w_in,
           ffn_conv_w, ffn_conv_b, ffn_w_out, ln2_g, ln2_b):
    p = _prepare_params(ln_in_g, ln_in_b, w_in, rnn_conv_w, rnn_conv_b, lru_w_a, lru_b_a, lru_w_x, lru_b_x,
                        lru_lambda, w_o_rnn, q_norm_g, w_uq, kv_norm_g, w_ukv, w_o_attn, b_gate, w_out, ln1_g,
                        ln1_b, ffn_w_in, ffn_conv_w, ffn_conv_b, ffn_w_out, ln2_g, ln2_b)
    return _trunk(x_prompt, p), _trunk(x_sample, p)
```

```python
import functools

import jax
import jax.numpy as jnp
from jax import lax
from jax.experimental import pallas as pl
from jax.experimental.pallas import tpu as pltpu

D_MODEL = 1024
D_RNN = 1024
RNN_BLOCKS = 16
RNN_BLOCK_W = D_RNN // RNN_BLOCKS
LRU_C = 8.0
N_HEADS = 8
Q_LORA = 384
KV_LORA = 256
QK_NOPE = 128
QK_ROPE = 64
QK_DIM = QK_NOPE + QK_ROPE
V_HEAD = 128
ROPE_THETA = 10000.0
D_FF = 3072
LN_EPS = 1e-5
RMS_EPS = 1e-6
DEPTH = 1
ALPHA = (2.0 * DEPTH) ** 0.25
LOG2_E = 1.4426950408889634

SUBLANES = 8
LANES = 128
BF16_ROWS = 16
VMEM_LIMIT = 56 * 1024 * 1024

TM_PROJ = 512
TT_RNN = 512
RNN_GROUP = 256
TQ_ATTN = 512
TK_ATTN = TM_PROJ
TM_MIX = 512
TM_FFN = 512
FF_CHUNK = 512

BF16 = jnp.bfloat16
F32 = jnp.float32


def _layer_norm(x, g, b):
    mu = jnp.mean(x, axis=-1, keepdims=True)
    xc = x - mu
    var = jnp.mean(xc * xc, axis=-1, keepdims=True)
    return xc * lax.rsqrt(var + LN_EPS) * g + b


def _rms_norm(x, g):
    return x * lax.rsqrt(jnp.mean(x * x, axis=-1, keepdims=True) + RMS_EPS) * g


def _dot(a, b):
    return jnp.dot(a, b, preferred_element_type=F32)


def _const_spec(shape):
    nd = len(shape)
    return pl.BlockSpec(shape, lambda *_: (0,) * nd, pipeline_mode=pl.Buffered(1))


def _proj_kernel(x_ref, lng_ref, lnb_ref, w_rnn_ref, w_gate_ref, bgate_ref, w_q_ref, qg_ref, w_uq_ref,
                 w_kv_ref, kvg_ref, w_uk_ref, w_uvt_ref, w_kr_ref, cos_ref, sin_ref,
                 xr_ref, gg_ref, gate_ref, q_ref, k_ref, vt_ref):
    h = _layer_norm(x_ref[...], lng_ref[...], lnb_ref[...]).astype(BF16)

    rnn = _dot(h, w_rnn_ref[...])
    xr_ref[...] = rnn[:, :D_RNN].astype(BF16)
    gg_ref[...] = jax.nn.gelu(rnn[:, D_RNN:], approximate=True).astype(BF16)

    gate_ref[...] = jax.nn.sigmoid(_dot(h, w_gate_ref[...]) + bgate_ref[...]).astype(BF16)

    cos2 = cos_ref[...]
    sin2 = sin_ref[...]
    cos_q = jnp.concatenate([cos2] * (N_HEADS * QK_ROPE // LANES), axis=1)
    sin_q = jnp.concatenate([sin2] * (N_HEADS * QK_ROPE // LANES), axis=1)

    qn = _rms_norm(_dot(h, w_q_ref[...]), qg_ref[...]).astype(BF16)
    qf = _dot(qn, w_uq_ref[...])
    n_nope = N_HEADS * QK_NOPE
    n_rope = N_HEADS * QK_ROPE
    scale = QK_DIM ** -0.5 * LOG2_E
    q_nope = qf[:, :n_nope] * scale
    q_rope = (qf[:, n_nope:n_nope + n_rope] * cos_q + qf[:, n_nope + n_rope:] * sin_q) * scale

    kvn = _rms_norm(_dot(h, w_kv_ref[...]), kvg_ref[...])
    k_nope = _dot(kvn.astype(BF16), w_uk_ref[...])
    vt = _dot(w_uvt_ref[...], jnp.transpose(kvn).astype(BF16))
    kr = _dot(h, w_kr_ref[...])
    k_rope = (kr[:, :QK_ROPE] * cos2[:, :QK_ROPE] + kr[:, QK_ROPE:] * sin2[:, :QK_ROPE]).astype(BF16)

    for hd in range(N_HEADS):
        q_ref[hd] = jnp.concatenate(
            [q_nope[:, hd * QK_NOPE:(hd + 1) * QK_NOPE], q_rope[:, hd * QK_ROPE:(hd + 1) * QK_ROPE]],
            axis=1).astype(BF16)
        k_ref[hd] = jnp.concatenate([k_nope[:, hd * QK_NOPE:(hd + 1) * QK_NOPE].astype(BF16), k_rope], axis=1)
        vt_ref[hd, 0] = vt[hd * V_HEAD:(hd + 1) * V_HEAD, :].astype(BF16)


def _proj(x2, seq, p):
    n = x2.shape[0]
    tm = TM_PROJ
    nt_seq = seq // tm
    row = lambda i: (i, 0)
    pos = lambda i: (i % nt_seq, 0)
    head_row = lambda i: (0, i, 0)
    consts = [p['ln_in_g'], p['ln_in_b'], p['w_rnn'], p['w_gate'], p['b_gate'], p['w_q'], p['q_norm_g'],
              p['w_uq'], p['w_kv'], p['kv_norm_g'], p['w_uk'], p['w_uvt'], p['w_kr']]
    in_specs = ([pl.BlockSpec((tm, D_MODEL), row)] + [_const_spec(c.shape) for c in consts]
                + [pl.BlockSpec((tm, LANES), pos), pl.BlockSpec((tm, LANES), pos)])
    out_shape = [
        jax.ShapeDtypeStruct((n, D_RNN), BF16),
        jax.ShapeDtypeStruct((n, D_RNN), BF16),
        jax.ShapeDtypeStruct((n, 2 * D_MODEL), BF16),
        jax.ShapeDtypeStruct((N_HEADS, n, QK_DIM), BF16),
        jax.ShapeDtypeStruct((N_HEADS, n, QK_DIM), BF16),
        jax.ShapeDtypeStruct((N_HEADS, n // tm, V_HEAD, tm), BF16),
    ]
    out_specs = [
        pl.BlockSpec((tm, D_RNN), row), pl.BlockSpec((tm, D_RNN), row), pl.BlockSpec((tm, 2 * D_MODEL), row),
        pl.BlockSpec((N_HEADS, tm, QK_DIM), head_row), pl.BlockSpec((N_HEADS, tm, QK_DIM), head_row),
        pl.BlockSpec((N_HEADS, 1, V_HEAD, tm), lambda i: (0, i, 0, 0)),
    ]
    return pl.pallas_call(
        _proj_kernel, grid=(n // tm,), in_specs=in_specs, out_specs=out_specs, out_shape=out_shape,
        compiler_params=pltpu.CompilerParams(dimension_semantics=("parallel",), vmem_limit_bytes=VMEM_LIMIT),
        name="proj",
    )(x2, *consts, p['cos2'], p['sin2'])


def _rnn_kernel(xf_ref, xfp_ref, xfn_ref, xb_ref, xbp_ref, xbn_ref, cw_ref, cb_ref, wg_ref, ba_ref, bx_ref,
                lam_ref, hf_ref, hb_ref, xe_ref, a_ref, b_ref, carry_ref, *, n_tiles):
    j = pl.program_id(1)
    tt = TT_RNN
    halo = BF16_ROWS

    @pl.when(j == 0)
    def _():
        carry_ref[...] = jnp.zeros_like(carry_ref)

    sub = lax.broadcasted_iota(jnp.int32, (SUBLANES, D_RNN), 0)

    def one_direction(d, x_ref, xp_ref, xn_ref, tile, out_ref):
        prev = jnp.where(tile > 0, xp_ref[0].astype(F32), 0.0)
        nxt = jnp.where(tile < n_tiles - 1, xn_ref[0].astype(F32), 0.0)
        xe_ref[0:halo, :] = prev
        xe_ref[halo:halo + tt, :] = x_ref[0].astype(F32)
        xe_ref[halo + tt:2 * halo + tt, :] = nxt
        u = cb_ref[...] + cw_ref[0:1, :] * xe_ref[halo - 2:halo - 2 + tt, :]
        for k in range(1, 4):
            u = u + cw_ref[k:k + 1, :] * xe_ref[halo - 2 + k:halo - 2 + k + tt, :]

        sp = jax.nn.softplus(-lam_ref[d:d + 1, :])
        for g in range(D_RNN // RNN_GROUP):
            cs = slice(g * RNN_GROUP, (g + 1) * RNN_GROUP)
            ug = u[:, cs]
            z = _dot(ug.astype(BF16), wg_ref[d, g])
            r = jax.nn.sigmoid(z[:, :RNN_GROUP] + ba_ref[d:d + 1, cs])
            i = jax.nn.sigmoid(z[:, RNN_GROUP:] + bx_ref[d:d + 1, cs])
            log_a = -LRU_C * r * sp[:, cs]
            a = jnp.exp(log_a)
            a_ref[:, cs] = a
            b_ref[:, cs] = jnp.sqrt(1.0 - a * a) * (i * ug)

        n_groups = tt // SUBLANES
        reverse = d == 1

        def body(gi, carry):
            grp = (n_groups - 1 - gi) if reverse else gi
            r0 = pl.multiple_of(grp * SUBLANES, SUBLANES)
            av = a_ref[pl.ds(r0, SUBLANES), :]
            bv = b_ref[pl.ds(r0, SUBLANES), :]
            for s in (1, 2, 4):
                if reverse:
                    m = sub < SUBLANES - s
                    sh = SUBLANES - s
                else:
                    m = sub >= s
                    sh = s
                ar = jnp.where(m, pltpu.roll(av, sh, 0), 1.0)
                br = jnp.where(m, pltpu.roll(bv, sh, 0), 0.0)
                bv = av * br + bv
                av = av * ar
            hv = av * carry + bv
            b_ref[pl.ds(r0, SUBLANES), :] = hv
            edge = hv[0:1, :] if reverse else hv[SUBLANES - 1:SUBLANES, :]
            return jnp.broadcast_to(edge, (SUBLANES, D_RNN))

        carry_ref[d] = lax.fori_loop(0, n_groups, body, carry_ref[d], unroll=2)
        out_ref[0] = b_ref[...].astype(BF16)

    one_direction(0, xf_ref, xfp_ref, xfn_ref, j, hf_ref)
    one_direction(1, xb_ref, xbp_ref, xbn_ref, n_tiles - 1 - j, hb_ref)


def _rnn(xr3, p):
    bsz, seq, _ = xr3.shape
    tt = TT_RNN
    nt = seq // tt
    hpt = tt // BF16_ROWS
    n_halo = seq // BF16_ROWS
    fwd = lambda b, j: (b, j, 0)
    fwd_p = lambda b, j: (b, jnp.maximum(j * hpt - 1, 0), 0)
    fwd_n = lambda b, j: (b, jnp.minimum((j + 1) * hpt, n_halo - 1), 0)
    bwd = lambda b, j: (b, nt - 1 - j, 0)
    bwd_p = lambda b, j: (b, jnp.maximum((nt - 1 - j) * hpt - 1, 0), 0)
    bwd_n = lambda b, j: (b, jnp.minimum((nt - j) * hpt, n_halo - 1), 0)
    consts = [p['rnn_conv_w'], p['rnn_conv_b'], p['w_lru'], p['lru_b_a'], p['lru_b_x'], p['lru_lambda']]
    main = (1, tt, D_RNN)
    hal = (1, BF16_ROWS, D_RNN)
    in_specs = [pl.BlockSpec(main, fwd), pl.BlockSpec(hal, fwd_p), pl.BlockSpec(hal, fwd_n),
                pl.BlockSpec(main, bwd), pl.BlockSpec(hal, bwd_p), pl.BlockSpec(hal, bwd_n)]
    in_specs += [_const_spec(c.shape) for c in consts]
    out_shape = [jax.ShapeDtypeStruct((bsz, seq, D_RNN), BF16)] * 2
    out_specs = [pl.BlockSpec(main, fwd), pl.BlockSpec(main, bwd)]
    scratch = [pltpu.VMEM((tt + 2 * BF16_ROWS, D_RNN), F32), pltpu.VMEM((tt, D_RNN), F32),
               pltpu.VMEM((tt, D_RNN), F32), pltpu.VMEM((2, SUBLANES, D_RNN), F32)]
    return pl.pallas_call(
        functools.partial(_rnn_kernel, n_tiles=nt), grid=(bsz, nt), in_specs=in_specs, out_specs=out_specs,
        out_shape=out_shape, scratch_shapes=scratch,
        compiler_params=pltpu.CompilerParams(dimension_semantics=("parallel", "arbitrary"),
                                             vmem_limit_bytes=VMEM_LIMIT),
        name="rnn",
    )(xr3, xr3, xr3, xr3, xr3, xr3, *consts)


def _attn_kernel(q_ref, k_ref, vt_ref, o_ref, s_ref, p_ref, acc_ref, *, n_kv):
    tk = TK_ATTN
    q = q_ref[0]

    def scores(i, slot):
        r0 = pl.multiple_of(i * tk, tk)
        s = lax.dot_general(k_ref[0, pl.ds(r0, tk), :], q, (((1,), (1,)), ((), ())),
                            preferred_element_type=F32)
        s_ref[slot] = s
        return jnp.max(s, axis=0, keepdims=True)

    def exponentials(slot, m, l, cmax):
        m_new = jnp.maximum(m, cmax)
        alpha = jnp.exp2(m - m_new)
        pr = jnp.exp2(s_ref[slot] - m_new)
        p_ref[slot] = pr.astype(BF16)
        return m_new, alpha * l + jnp.sum(pr, axis=0, keepdims=True), alpha

    def values(i, slot, alpha):
        acc_ref[...] = alpha * acc_ref[...] + _dot(vt_ref[0, i], p_ref[slot])

    tq = q.shape[0]
    m = jnp.full((1, tq), -jnp.inf, F32)
    l = jnp.zeros((1, tq), F32)
    acc_ref[...] = jnp.zeros_like(acc_ref)
    cmax = scores(0, 0)
    cmax_next = scores(1, 1)
    m, l, alpha = exponentials(0, m, l, cmax)
    cmax = cmax_next

    alpha_prev = alpha
    for i in range(1, n_kv - 1):
        cur = i % 2
        cmax_next = scores(i + 1, 1 - cur)
        m, l, alpha = exponentials(cur, m, l, cmax)
        values(i - 1, 1 - cur, alpha_prev)
        cmax, alpha_prev = cmax_next, alpha
    last = n_kv - 1
    m, l, alpha = exponentials(last % 2, m, l, cmax)
    values(last - 1, 1 - last % 2, alpha_prev)
    values(last, last % 2, alpha)
    o_ref[...] = jnp.transpose(acc_ref[...] / l).astype(BF16)


def _attn(q, k, vt, bsz, seq):
    n = bsz * seq
    tq = TQ_ATTN
    nq = seq // tq
    n_kv = seq // TK_ATTN
    in_specs = [pl.BlockSpec((1, tq, QK_DIM), lambda b, h, i: (h, b * nq + i, 0)),
                pl.BlockSpec((1, seq, QK_DIM), lambda b, h, i: (h, b, 0)),
                pl.BlockSpec((1, n_kv, V_HEAD, TK_ATTN), lambda b, h, i: (h, b, 0, 0))]
    out_specs = pl.BlockSpec((tq, V_HEAD), lambda b, h, i: (b * nq + i, h))
    assert n_kv % 2 == 0 and n_kv >= 4
    scratch = [pltpu.VMEM((2, TK_ATTN, tq), F32), pltpu.VMEM((2, TK_ATTN, tq), BF16),
               pltpu.VMEM((V_HEAD, tq), F32)]
    return pl.pallas_call(
        functools.partial(_attn_kernel, n_kv=n_kv), grid=(bsz, N_HEADS, nq), in_specs=in_specs,
        out_specs=out_specs, out_shape=jax.ShapeDtypeStruct((n, N_HEADS * V_HEAD), BF16),
        scratch_shapes=scratch,
        compiler_params=pltpu.CompilerParams(dimension_semantics=("parallel", "parallel", "arbitrary"),
                                             vmem_limit_bytes=VMEM_LIMIT),
        name="attn",
    )(q, k, vt)


def _mix_kernel(x_ref, hf_ref, hb_ref, gg_ref, o_ref, gate_ref, lng_ref, lnb_ref, w_or_ref, w_oa_ref, w_out_ref,
                l1g_ref, l1b_ref, h1_ref):
    h = _layer_norm(x_ref[...], lng_ref[...], lnb_ref[...])
    rnn_in = (hf_ref[...].astype(F32) + hb_ref[...].astype(F32)) * gg_ref[...].astype(F32)
    y_rnn = _dot(rnn_in.astype(BF16), w_or_ref[...])
    y_attn = _dot(o_ref[...], w_oa_ref[...])
    gates = gate_ref[...].astype(F32)
    merged = gates[:, :D_MODEL] * y_rnn + gates[:, D_MODEL:] * y_attn
    mix = _dot(merged.astype(BF16), w_out_ref[...])
    h1_ref[...] = _layer_norm(ALPHA * h + mix, l1g_ref[...], l1b_ref[...])


def _mix(x2, hf, hb, gg, o, gate, p):
    n = x2.shape[0]
    tm = TM_MIX
    row = lambda i: (i, 0)
    acts = [x2, hf, hb, gg, o, gate]
    consts = [p['ln_in_g'], p['ln_in_b'], p['w_o_rnn'], p['w_o_attn'], p['w_out'], p['ln1_g'], p['ln1_b']]
    in_specs = [pl.BlockSpec((tm, a.shape[1]), row) for a in acts] + [_const_spec(c.shape) for c in consts]
    return pl.pallas_call(
        _mix_kernel, grid=(n // tm,), in_specs=in_specs, out_specs=pl.BlockSpec((tm, D_MODEL), row),
        out_shape=jax.ShapeDtypeStruct((n, D_MODEL), F32),
        compiler_params=pltpu.CompilerParams(dimension_semantics=("parallel",), vmem_limit_bytes=VMEM_LIMIT),
        name="mix",
    )(*acts, *consts)


def _ffn_kernel(h_ref, hp_ref, hn_ref, wg_ref, wu_ref, cw_ref, cb_ref, wo_ref, l2g_ref, l2b_ref, out_ref,
                ext_ref, g_ref, acc_ref, *, n_tiles):
    j = pl.program_id(1)
    tm = TM_FFN
    halo = SUBLANES
    hm = h_ref[0]
    ext_ref[0:halo, :] = jnp.where(j > 0, hp_ref[0], 0.0)
    ext_ref[halo:halo + tm, :] = hm
    ext_ref[halo + tm:2 * halo + tm, :] = jnp.where(j < n_tiles - 1, hn_ref[0], 0.0)
    ext = ext_ref[...].astype(BF16)
    hmb = hm.astype(BF16)
    acc_ref[...] = jnp.zeros_like(acc_ref)

    def body(c, _):
        g_ref[...] = _dot(ext, wg_ref[c])
        cw = cw_ref[c]
        gc = cb_ref[c] + cw[0:1, :] * g_ref[halo - 1:halo - 1 + tm, :]
        gc = gc + cw[1:2, :] * g_ref[halo:halo + tm, :]
        gc = gc + cw[2:3, :] * g_ref[halo + 1:halo + 1 + tm, :]
        up = _dot(hmb, wu_ref[c])
        act = (jax.nn.gelu(gc, approximate=True) * up).astype(BF16)
        acc_ref[...] += _dot(act, wo_ref[c])
        return 0

    lax.fori_loop(0, D_FF // FF_CHUNK, body, 0)
    out_ref[0] = _layer_norm(ALPHA * hm + acc_ref[...], l2g_ref[...], l2b_ref[...])


def _ffn(h1, p):
    bsz, seq, _ = h1.shape
    tm = TM_FFN
    nt = seq // tm
    hpt = tm // SUBLANES
    n_halo = seq // SUBLANES
    main = (1, tm, D_MODEL)
    hal = (1, SUBLANES, D_MODEL)
    cur = lambda b, j: (b, j, 0)
    prv = lambda b, j: (b, jnp.maximum(j * hpt - 1, 0), 0)
    nxt = lambda b, j: (b, jnp.minimum((j + 1) * hpt, n_halo - 1), 0)
    consts = [p['ffn_w_gate'], p['ffn_w_up'], p['ffn_conv_w'], p['ffn_conv_b'], p['ffn_w_out'], p['ln2_g'],
              p['ln2_b']]
    in_specs = [pl.BlockSpec(main, cur), pl.BlockSpec(hal, prv), pl.BlockSpec(hal, nxt)]
    in_specs += [_const_spec(c.shape) for c in consts]
    scratch = [pltpu.VMEM((tm + 2 * SUBLANES, D_MODEL), F32), pltpu.VMEM((tm + 2 * SUBLANES, FF_CHUNK), F32),
               pltpu.VMEM((tm, D_MODEL), F32)]
    return pl.pallas_call(
        functools.partial(_ffn_kernel, n_tiles=nt), grid=(bsz, nt), in_specs=in_specs,
        out_specs=pl.BlockSpec(main, cur), out_shape=jax.ShapeDtypeStruct((bsz, seq, D_MODEL), F32),
        scratch_shapes=scratch,
        compiler_params=pltpu.CompilerParams(dimension_semantics=("parallel", "parallel"),
                                             vmem_limit_bytes=VMEM_LIMIT),
        name="ffn",
    )(h1, h1, h1, *consts)


def _rot_cols(w):
    half = QK_ROPE // 2
    return jnp.concatenate([-w[..., half:], w[..., :half]], axis=-1)


def _block_diag_groups(w):
    per = RNN_GROUP // RNN_BLOCK_W
    w4 = w.reshape(D_RNN // RNN_GROUP, per, RNN_BLOCK_W, RNN_BLOCK_W)
    eye = jnp.eye(per, dtype=w.dtype)
    return jnp.einsum('gkij,kl->gkilj', w4, eye).reshape(D_RNN // RNN_GROUP, RNN_GROUP, RNN_GROUP)


def _prepare_params(ln_in_g, ln_in_b, w_in, rnn_conv_w, rnn_conv_b, lru_w_a, lru_b_a, lru_w_x, lru_b_x, lru_lambda,
                    w_o_rnn, q_norm_g, w_uq, kv_norm_g, w_ukv, w_o_attn, b_gate, w_out, ln1_g, ln1_b, ffn_w_in,
                    ffn_conv_w, ffn_conv_b, ffn_w_out, ln2_g, ln2_b):
    l = 0
    row = lambda v: v.reshape(1, -1).astype(F32)
    w = w_in[l]
    c0, c1, c2, c3 = 2 * D_RNN, 2 * D_RNN + Q_LORA, 2 * D_RNN + Q_LORA + KV_LORA, 2 * D_RNN + Q_LORA + KV_LORA + QK_ROPE
    w_kr = w[:, c2:c3]
    uq = w_uq[l].reshape(Q_LORA, N_HEADS, QK_DIM)
    uq_nope = uq[:, :, :QK_NOPE].reshape(Q_LORA, N_HEADS * QK_NOPE)
    uq_rope = uq[:, :, QK_NOPE:]
    ukv = w_ukv[l].reshape(KV_LORA, N_HEADS, QK_NOPE + V_HEAD)
    n_chunks = D_FF // FF_CHUNK
    fw = ffn_w_in[l]
    chunk_cols = lambda m: m.reshape(m.shape[0], n_chunks, FF_CHUNK).transpose(1, 0, 2)
    return {
        'ln_in_g': row(ln_in_g), 'ln_in_b': row(ln_in_b),
        'w_rnn': w[:, :c0].astype(BF16),
        'w_q': w[:, c0:c1].astype(BF16),
        'w_kv': w[:, c1:c2].astype(BF16),
        'w_kr': jnp.concatenate([w_kr, _rot_cols(w_kr)], axis=1).astype(BF16),
        'w_gate': w[:, c3:].astype(BF16),
        'b_gate': row(b_gate[l]),
        'q_norm_g': row(q_norm_g[l]), 'kv_norm_g': row(kv_norm_g[l]),
        'w_uq': jnp.concatenate([uq_nope, uq_rope.reshape(Q_LORA, -1), _rot_cols(uq_rope).reshape(Q_LORA, -1)],
                                axis=1).astype(BF16),
        'w_uk': ukv[:, :, :QK_NOPE].reshape(KV_LORA, N_HEADS * QK_NOPE).astype(BF16),
        'w_uvt': ukv[:, :, QK_NOPE:].reshape(KV_LORA, N_HEADS * V_HEAD).T.astype(BF16),
        'rnn_conv_w': rnn_conv_w[l].astype(F32), 'rnn_conv_b': row(rnn_conv_b[l]),
        'w_lru': jnp.stack([jnp.concatenate([_block_diag_groups(lru_w_a[l, d]), _block_diag_groups(lru_w_x[l, d])],
                                            axis=2) for d in range(2)]).astype(BF16),
        'lru_b_a': lru_b_a[l].astype(F32), 'lru_b_x': lru_b_x[l].astype(F32), 'lru_lambda': lru_lambda[l].astype(F32),
        'w_o_rnn': w_o_rnn[l].astype(BF16), 'w_o_attn': w_o_attn[l].astype(BF16), 'w_out': w_out[l].astype(BF16),
        'ln1_g': row(ln1_g[l]), 'ln1_b': row(ln1_b[l]),
        'ffn_w_gate': chunk_cols(fw[:, :D_FF]).astype(BF16),
        'ffn_w_up': chunk_cols(fw[:, D_FF:]).astype(BF16),
        'ffn_conv_w': chunk_cols(ffn_conv_w[l]).astype(F32),
        'ffn_conv_b': ffn_conv_b[l].reshape(n_chunks, 1, FF_CHUNK).astype(F32),
        'ffn_w_out': ffn_w_out[l].reshape(n_chunks, FF_CHUNK, D_MODEL).astype(BF16),
        'ln2_g': row(ln2_g[l]), 'ln2_b': row(ln2_b[l]),
    }


def _rope_tables(seq):
    pos = jnp.arange(seq, dtype=F32)
    inv = ROPE_THETA ** (-jnp.arange(0, QK_ROPE, 2, dtype=F32) / QK_ROPE)
    ang = pos[:, None] * inv[None, :]
    reps = LANES // (QK_ROPE // 2)
    return jnp.tile(jnp.cos(ang), (1, reps)), jnp.tile(jnp.sin(ang), (1, reps))


def _trunk(x, p):
    bsz, seq, _ = x.shape
    n = bsz * seq
    x2 = x.reshape(n, D_MODEL)
    cos2, sin2 = _rope_tables(seq)
    pp = dict(p, cos2=cos2, sin2=sin2)
    xr, gg, gate, q, k, v = _proj(x2, seq, pp)
    hf, hb = _rnn(xr.reshape(bsz, seq, D_RNN), p)
    o = _attn(q, k, v, bsz, seq)
    h1 = _mix(x2, hf.reshape(n, D_RNN), hb.reshape(n, D_RNN), gg, o, gate, p)
    return _ffn(h1.reshape(bsz, seq, D_MODEL), p)


def kernel(x_prompt, x_sample, ln_in_g, ln_in_b, w_in, rnn_conv_w, rnn_conv_b, lru_w_a, lru_b_a, lru_w_x, lru_b_x,
           lru_lambda, w_o_rnn, q_norm_g, w_uq, kv_norm_g, w_ukv, w_o_attn, b_gate, w_out, ln1_g, ln1_b, ffn_w_in,
           ffn_conv_w, ffn_conv_b, ffn_w_out, ln2_g, ln2_b):
    p = _prepare_params(ln_in_g, ln_in_b, w_in, rnn_conv_w, rnn_conv_b, lru_w_a, lru_b_a, lru_w_x, lru_b_x,
                        lru_lambda, w_o_rnn, q_norm_g, w_uq, kv_norm_g, w_ukv, w_o_attn, b_gate, w_out, ln1_g,
                        ln1_b, ffn_w_in, ffn_conv_w, ffn_conv_b, ffn_w_out, ln2_g, ln2_b)
    return _trunk(x_prompt, p), _trunk(x_sample, p)
```

```python
import functools

import jax
import jax.numpy as jnp
from jax import lax
from jax.experimental import pallas as pl
from jax.experimental.pallas import tpu as pltpu

D_MODEL = 1024
D_RNN = 1024
RNN_BLOCKS = 16
RNN_BLOCK_W = D_RNN // RNN_BLOCKS
LRU_C = 8.0
N_HEADS = 8
Q_LORA = 384
KV_LORA = 256
QK_NOPE = 128
QK_ROPE = 64
QK_DIM = QK_NOPE + QK_ROPE
V_HEAD = 128
ROPE_THETA = 10000.0
D_FF = 3072
LN_EPS = 1e-5
RMS_EPS = 1e-6
DEPTH = 1
ALPHA = (2.0 * DEPTH) ** 0.25
LOG2_E = 1.4426950408889634
SQRT_FLOOR = 1e-30

SUBLANES = 8
LANES = 128
BF16_ROWS = 16
VMEM_LIMIT = 56 * 1024 * 1024

TM_PROJ = 512
TT_RNN = 512
RNN_GROUP = 256
TQ_ATTN = 512
TK_ATTN = TM_PROJ
ATTN_ROWS = 32
TM_MIX = 512
TM_FFN = 512
FF_CHUNK = 512

BF16 = jnp.bfloat16
F32 = jnp.float32


def _layer_norm(x, g, b):
    mu = jnp.mean(x, axis=-1, keepdims=True)
    xc = x - mu
    var = jnp.mean(xc * xc, axis=-1, keepdims=True)
    return xc * lax.rsqrt(var + LN_EPS) * g + b


def _rms_norm(x, g):
    return x * lax.rsqrt(jnp.mean(x * x, axis=-1, keepdims=True) + RMS_EPS) * g


def _sigmoid(x):
    return 0.5 * jnp.tanh(0.5 * x) + 0.5


def _dot(a, b):
    return jnp.dot(a, b, preferred_element_type=F32)


def _const_spec(shape):
    nd = len(shape)
    return pl.BlockSpec(shape, lambda *_: (0,) * nd, pipeline_mode=pl.Buffered(1))


def _proj_kernel(x_ref, lng_ref, lnb_ref, w_rnn_ref, w_gate_ref, bgate_ref, w_lat_ref, qg_ref, w_uq_ref,
                 kvg_ref, w_uk_ref, w_uvt_ref, cos_ref, sin_ref,
                 xr_ref, gg_ref, gate_ref, q_ref, k_ref, vt_ref):
    h = _layer_norm(x_ref[...], lng_ref[...], lnb_ref[...]).astype(BF16)

    rnn = _dot(h, w_rnn_ref[...])
    xr_ref[...] = rnn[:, :D_RNN].astype(BF16)
    gg_ref[...] = jax.nn.gelu(rnn[:, D_RNN:], approximate=True).astype(BF16)

    gate_ref[...] = _sigmoid(_dot(h, w_gate_ref[...]) + bgate_ref[...]).astype(BF16)

    cos2 = cos_ref[...]
    sin2 = sin_ref[...]
    cos_q = jnp.concatenate([cos2] * (N_HEADS * QK_ROPE // LANES), axis=1)
    sin_q = jnp.concatenate([sin2] * (N_HEADS * QK_ROPE // LANES), axis=1)

    lat = _dot(h, w_lat_ref[...])
    qn = _rms_norm(lat[:, :Q_LORA], qg_ref[...]).astype(BF16)
    qf = _dot(qn, w_uq_ref[...])
    n_nope = N_HEADS * QK_NOPE
    n_rope = N_HEADS * QK_ROPE
    scale = QK_DIM ** -0.5 * LOG2_E
    q_nope = qf[:, :n_nope] * scale
    q_rope = (qf[:, n_nope:n_nope + n_rope] * cos_q + qf[:, n_nope + n_rope:] * sin_q) * scale

    kvn = _rms_norm(lat[:, Q_LORA:Q_LORA + KV_LORA], kvg_ref[...])
    k_nope = _dot(kvn.astype(BF16), w_uk_ref[...])
    vt = _dot(w_uvt_ref[...], jnp.transpose(kvn).astype(BF16))
    kr = lat[:, Q_LORA + KV_LORA:]
    k_rope = (kr[:, :QK_ROPE] * cos2[:, :QK_ROPE] + kr[:, QK_ROPE:] * sin2[:, :QK_ROPE]).astype(BF16)

    for hd in range(N_HEADS):
        q_ref[hd] = jnp.concatenate(
            [q_nope[:, hd * QK_NOPE:(hd + 1) * QK_NOPE], q_rope[:, hd * QK_ROPE:(hd + 1) * QK_ROPE]],
            axis=1).astype(BF16)
        k_ref[hd] = jnp.concatenate([k_nope[:, hd * QK_NOPE:(hd + 1) * QK_NOPE].astype(BF16), k_rope], axis=1)
        vt_ref[hd, 0] = vt[hd * V_HEAD:(hd + 1) * V_HEAD, :].astype(BF16)


def _proj(x2, seq, p):
    n = x2.shape[0]
    tm = TM_PROJ
    nt_seq = seq // tm
    row = lambda i: (i, 0)
    pos = lambda i: (i % nt_seq, 0)
    head_row = lambda i: (0, i, 0)
    consts = [p['ln_in_g'], p['ln_in_b'], p['w_rnn'], p['w_gate'], p['b_gate'], p['w_lat'], p['q_norm_g'],
              p['w_uq'], p['kv_norm_g'], p['w_uk'], p['w_uvt']]
    in_specs = ([pl.BlockSpec((tm, D_MODEL), row)] + [_const_spec(c.shape) for c in consts]
                + [pl.BlockSpec((tm, LANES), pos), pl.BlockSpec((tm, LANES), pos)])
    out_shape = [
        jax.ShapeDtypeStruct((n, D_RNN), BF16),
        jax.ShapeDtypeStruct((n, D_RNN), BF16),
        jax.ShapeDtypeStruct((n, 2 * D_MODEL), BF16),
        jax.ShapeDtypeStruct((N_HEADS, n, QK_DIM), BF16),
        jax.ShapeDtypeStruct((N_HEADS, n, QK_DIM), BF16),
        jax.ShapeDtypeStruct((N_HEADS, n // tm, V_HEAD, tm), BF16),
    ]
    out_specs = [
        pl.BlockSpec((tm, D_RNN), row), pl.BlockSpec((tm, D_RNN), row), pl.BlockSpec((tm, 2 * D_MODEL), row),
        pl.BlockSpec((N_HEADS, tm, QK_DIM), head_row), pl.BlockSpec((N_HEADS, tm, QK_DIM), head_row),
        pl.BlockSpec((N_HEADS, 1, V_HEAD, tm), lambda i: (0, i, 0, 0)),
    ]
    return pl.pallas_call(
        _proj_kernel, grid=(n // tm,), in_specs=in_specs, out_specs=out_specs, out_shape=out_shape,
        compiler_params=pltpu.CompilerParams(dimension_semantics=("parallel",), vmem_limit_bytes=VMEM_LIMIT),
        name="proj",
    )(x2, *consts, p['cos2'], p['sin2'])


def _rnn_kernel(xf_ref, xfp_ref, xfn_ref, xb_ref, xbp_ref, xbn_ref, cw_ref, cb_ref, wg_ref, ba_ref, bx_ref,
                lam_ref, hf_ref, hb_ref, xe_ref, a_ref, b_ref, carry_ref, *, n_tiles):
    j = pl.program_id(1)
    tt = TT_RNN
    halo = BF16_ROWS

    @pl.when(j == 0)
    def _():
        carry_ref[...] = jnp.zeros_like(carry_ref)

    sub = lax.broadcasted_iota(jnp.int32, (SUBLANES, D_RNN), 0)

    def one_direction(d, x_ref, xp_ref, xn_ref, tile, out_ref):
        prev = jnp.where(tile > 0, xp_ref[0].astype(F32), 0.0)
        nxt = jnp.where(tile < n_tiles - 1, xn_ref[0].astype(F32), 0.0)
        xe_ref[0:halo, :] = prev
        xe_ref[halo:halo + tt, :] = x_ref[0].astype(F32)
        xe_ref[halo + tt:2 * halo + tt, :] = nxt
        u = cb_ref[...] + cw_ref[0:1, :] * xe_ref[halo - 2:halo - 2 + tt, :]
        for k in range(1, 4):
            u = u + cw_ref[k:k + 1, :] * xe_ref[halo - 2 + k:halo - 2 + k + tt, :]

        sp = jax.nn.softplus(-lam_ref[d:d + 1, :])
        for g in range(D_RNN // RNN_GROUP):
            cs = slice(g * RNN_GROUP, (g + 1) * RNN_GROUP)
            ug = u[:, cs]
            z = _dot(ug.astype(BF16), wg_ref[d, g])
            r = _sigmoid(z[:, :RNN_GROUP] + ba_ref[d:d + 1, cs])
            i = _sigmoid(z[:, RNN_GROUP:] + bx_ref[d:d + 1, cs])
            log_a = -LRU_C * r * sp[:, cs]
            a = jnp.exp(log_a)
            a_ref[:, cs] = a
            y = 1.0 - a * a
            b_ref[:, cs] = y * lax.rsqrt(jnp.maximum(y, SQRT_FLOOR)) * (i * ug)

        n_groups = tt // SUBLANES
        reverse = d == 1

        def body(gi, carry):
            grp = (n_groups - 1 - gi) if reverse else gi
            r0 = pl.multiple_of(grp * SUBLANES, SUBLANES)
            av = a_ref[pl.ds(r0, SUBLANES), :]
            bv = b_ref[pl.ds(r0, SUBLANES), :]
            for s in (1, 2, 4):
                if reverse:
                    m = sub < SUBLANES - s
                    sh = SUBLANES - s
                else:
                    m = sub >= s
                    sh = s
                ar = jnp.where(m, pltpu.roll(av, sh, 0), 1.0)
                br = jnp.where(m, pltpu.roll(bv, sh, 0), 0.0)
                bv = av * br + bv
                av = av * ar
            hv = av * carry + bv
            b_ref[pl.ds(r0, SUBLANES), :] = hv
            edge = hv[0:1, :] if reverse else hv[SUBLANES - 1:SUBLANES, :]
            return jnp.broadcast_to(edge, (SUBLANES, D_RNN))

        carry_ref[d] = lax.fori_loop(0, n_groups, body, carry_ref[d], unroll=2)
        out_ref[0] = b_ref[...].astype(BF16)

    one_direction(0, xf_ref, xfp_ref, xfn_ref, j, hf_ref)
    one_direction(1, xb_ref, xbp_ref, xbn_ref, n_tiles - 1 - j, hb_ref)


def _rnn(xr3, p):
    bsz, seq, _ = xr3.shape
    tt = TT_RNN
    nt = seq // tt
    hpt = tt // BF16_ROWS
    n_halo = seq // BF16_ROWS
    fwd = lambda b, j: (b, j, 0)
    fwd_p = lambda b, j: (b, jnp.maximum(j * hpt - 1, 0), 0)
    fwd_n = lambda b, j: (b, jnp.minimum((j + 1) * hpt, n_halo - 1), 0)
    bwd = lambda b, j: (b, nt - 1 - j, 0)
    bwd_p = lambda b, j: (b, jnp.maximum((nt - 1 - j) * hpt - 1, 0), 0)
    bwd_n = lambda b, j: (b, jnp.minimum((nt - j) * hpt, n_halo - 1), 0)
    consts = [p['rnn_conv_w'], p['rnn_conv_b'], p['w_lru'], p['lru_b_a'], p['lru_b_x'], p['lru_lambda']]
    main = (1, tt, D_RNN)
    hal = (1, BF16_ROWS, D_RNN)
    in_specs = [pl.BlockSpec(main, fwd), pl.BlockSpec(hal, fwd_p), pl.BlockSpec(hal, fwd_n),
                pl.BlockSpec(main, bwd), pl.BlockSpec(hal, bwd_p), pl.BlockSpec(hal, bwd_n)]
    in_specs += [_const_spec(c.shape) for c in consts]
    out_shape = [jax.ShapeDtypeStruct((bsz, seq, D_RNN), BF16)] * 2
    out_specs = [pl.BlockSpec(main, fwd), pl.BlockSpec(main, bwd)]
    scratch = [pltpu.VMEM((tt + 2 * BF16_ROWS, D_RNN), F32), pltpu.VMEM((tt, D_RNN), F32),
               pltpu.VMEM((tt, D_RNN), F32), pltpu.VMEM((2, SUBLANES, D_RNN), F32)]
    return pl.pallas_call(
        functools.partial(_rnn_kernel, n_tiles=nt), grid=(bsz, nt), in_specs=in_specs, out_specs=out_specs,
        out_shape=out_shape, scratch_shapes=scratch,
        compiler_params=pltpu.CompilerParams(dimension_semantics=("parallel", "arbitrary"),
                                             vmem_limit_bytes=VMEM_LIMIT),
        name="rnn",
    )(xr3, xr3, xr3, xr3, xr3, xr3, *consts)


def _attn_kernel(q_ref, k_ref, vt_ref, o_ref, s_ref, p_ref, acc_ref, *, n_kv):
    tk = TK_ATTN
    qt = jnp.transpose(q_ref[0].astype(F32)).astype(BF16)

    def scores(i, slot):
        r0 = pl.multiple_of(i * tk, tk)
        s = _dot(k_ref[0, pl.ds(r0, tk), :], qt)
        s_ref[slot] = s
        return jnp.max(s, axis=0, keepdims=True)

    def exponentials(slot, m, l, cmax):
        m_new = jnp.maximum(m, cmax)
        alpha = jnp.exp2(m - m_new)
        m_rows = jnp.broadcast_to(m_new, (ATTN_ROWS, tq))
        part = jnp.zeros((SUBLANES, tq), F32)
        for r in range(0, tk, ATTN_ROWS):
            pr = jnp.exp2(s_ref[slot, r:r + ATTN_ROWS, :] - m_rows)
            p_ref[slot, r:r + ATTN_ROWS, :] = pr.astype(BF16)
            part = part + jnp.sum(pr.reshape(ATTN_ROWS // SUBLANES, SUBLANES, tq), axis=0)
        return m_new, alpha * l + jnp.sum(part, axis=0, keepdims=True), alpha

    def values(i, slot, alpha):
        acc_ref[...] = alpha * acc_ref[...] + _dot(vt_ref[0, i], p_ref[slot])

    tq = qt.shape[1]
    m = jnp.full((1, tq), -jnp.inf, F32)
    l = jnp.zeros((1, tq), F32)
    acc_ref[...] = jnp.zeros_like(acc_ref)
    cmax = scores(0, 0)
    cmax_next = scores(1, 1)
    m, l, alpha = exponentials(0, m, l, cmax)
    cmax = cmax_next

    alpha_prev = alpha
    for i in range(1, n_kv - 1):
        cur = i % 2
        cmax_next = scores(i + 1, 1 - cur)
        m, l, alpha = exponentials(cur, m, l, cmax)
        values(i - 1, 1 - cur, alpha_prev)
        cmax, alpha_prev = cmax_next, alpha
    last = n_kv - 1
    m, l, alpha = exponentials(last % 2, m, l, cmax)
    values(last - 1, 1 - last % 2, alpha_prev)
    values(last, last % 2, alpha)
    o_ref[...] = jnp.transpose(acc_ref[...] / l).astype(BF16)


def _attn(q, k, vt, bsz, seq):
    n = bsz * seq
    tq = TQ_ATTN
    nq = seq // tq
    n_kv = seq // TK_ATTN
    in_specs = [pl.BlockSpec((1, tq, QK_DIM), lambda b, h, i: (h, b * nq + i, 0)),
                pl.BlockSpec((1, seq, QK_DIM), lambda b, h, i: (h, b, 0)),
                pl.BlockSpec((1, n_kv, V_HEAD, TK_ATTN), lambda b, h, i: (h, b, 0, 0))]
    out_specs = pl.BlockSpec((tq, V_HEAD), lambda b, h, i: (b * nq + i, h))
    assert n_kv % 2 == 0 and n_kv >= 4
    scratch = [pltpu.VMEM((2, TK_ATTN, tq), F32), pltpu.VMEM((2, TK_ATTN, tq), BF16),
               pltpu.VMEM((V_HEAD, tq), F32)]
    return pl.pallas_call(
        functools.partial(_attn_kernel, n_kv=n_kv), grid=(bsz, N_HEADS, nq), in_specs=in_specs,
        out_specs=out_specs, out_shape=jax.ShapeDtypeStruct((n, N_HEADS * V_HEAD), BF16),
        scratch_shapes=scratch,
        compiler_params=pltpu.CompilerParams(dimension_semantics=("parallel", "parallel", "arbitrary"),
                                             vmem_limit_bytes=VMEM_LIMIT),
        name="attn",
    )(q, k, vt)


def _mix_kernel(x_ref, hf_ref, hb_ref, gg_ref, o_ref, gate_ref, lng_ref, lnb_ref, w_or_ref, w_oa_ref, w_out_ref,
                l1g_ref, l1b_ref, h1_ref):
    h = _layer_norm(x_ref[...], lng_ref[...], lnb_ref[...])
    rnn_in = (hf_ref[...].astype(F32) + hb_ref[...].astype(F32)) * gg_ref[...].astype(F32)
    y_rnn = _dot(rnn_in.astype(BF16), w_or_ref[...])
    y_attn = _dot(o_ref[...], w_oa_ref[...])
    gates = gate_ref[...].astype(F32)
    merged = gates[:, :D_MODEL] * y_rnn + gates[:, D_MODEL:] * y_attn
    mix = _dot(merged.astype(BF16), w_out_ref[...])
    h1_ref[...] = _layer_norm(ALPHA * h + mix, l1g_ref[...], l1b_ref[...])


def _mix(x2, hf, hb, gg, o, gate, p):
    n = x2.shape[0]
    tm = TM_MIX
    row = lambda i: (i, 0)
    acts = [x2, hf, hb, gg, o, gate]
    consts = [p['ln_in_g'], p['ln_in_b'], p['w_o_rnn'], p['w_o_attn'], p['w_out'], p['ln1_g'], p['ln1_b']]
    in_specs = [pl.BlockSpec((tm, a.shape[1]), row) for a in acts] + [_const_spec(c.shape) for c in consts]
    return pl.pallas_call(
        _mix_kernel, grid=(n // tm,), in_specs=in_specs, out_specs=pl.BlockSpec((tm, D_MODEL), row),
        out_shape=jax.ShapeDtypeStruct((n, D_MODEL), F32),
        compiler_params=pltpu.CompilerParams(dimension_semantics=("parallel",), vmem_limit_bytes=VMEM_LIMIT),
        name="mix",
    )(*acts, *consts)


def _ffn_kernel(h_ref, hp_ref, hn_ref, wg_ref, wu_ref, cw_ref, cb_ref, wo_ref, l2g_ref, l2b_ref, out_ref,
                ext_ref, g_ref, u_ref, a_ref, *, n_tiles):
    j = pl.program_id(1)
    tm = TM_FFN
    halo = SUBLANES
    hm = h_ref[0]
    ext_ref[0:halo, :] = jnp.where(j > 0, hp_ref[0], 0.0)
    ext_ref[halo:halo + tm, :] = hm
    ext_ref[halo + tm:2 * halo + tm, :] = jnp.where(j < n_tiles - 1, hn_ref[0], 0.0)
    ext = ext_ref[...].astype(BF16)
    hmb = hm.astype(BF16)

    n_chunks = D_FF // FF_CHUNK

    def up_project(c):
        g_ref[c % 2] = _dot(ext, wg_ref[c])
        u_ref[c % 2] = _dot(hmb, wu_ref[c])

    def activate(c):
        g = g_ref.at[c % 2]
        cw = cw_ref[c]
        gc = cb_ref[c] + cw[0:1, :] * g[halo - 1:halo - 1 + tm, :]
        gc = gc + cw[1:2, :] * g[halo:halo + tm, :]
        gc = gc + cw[2:3, :] * g[halo + 1:halo + 1 + tm, :]
        a_ref[c % 2] = (jax.nn.gelu(gc, approximate=True) * u_ref[c % 2]).astype(BF16)

    def down_project(c, acc):
        part = _dot(a_ref[c % 2], wo_ref[c])
        return part if acc is None else acc + part

    acc = None
    up_project(0)
    for c in range(n_chunks):
        if c + 1 < n_chunks:
            up_project(c + 1)
        activate(c)
        if c >= 1:
            acc = down_project(c - 1, acc)
    acc = down_project(n_chunks - 1, acc)
    out_ref[0] = _layer_norm(ALPHA * hm + acc, l2g_ref[...], l2b_ref[...])


def _ffn(h1, p):
    bsz, seq, _ = h1.shape
    tm = TM_FFN
    nt = seq // tm
    hpt = tm // SUBLANES
    n_halo = seq // SUBLANES
    main = (1, tm, D_MODEL)
    hal = (1, SUBLANES, D_MODEL)
    cur = lambda b, j: (b, j, 0)
    prv = lambda b, j: (b, jnp.maximum(j * hpt - 1, 0), 0)
    nxt = lambda b, j: (b, jnp.minimum((j + 1) * hpt, n_halo - 1), 0)
    consts = [p['ffn_w_gate'], p['ffn_w_up'], p['ffn_conv_w'], p['ffn_conv_b'], p['ffn_w_out'], p['ln2_g'],
              p['ln2_b']]
    in_specs = [pl.BlockSpec(main, cur), pl.BlockSpec(hal, prv), pl.BlockSpec(hal, nxt)]
    in_specs += [_const_spec(c.shape) for c in consts]
    scratch = [pltpu.VMEM((tm + 2 * SUBLANES, D_MODEL), F32), pltpu.VMEM((2, tm + 2 * SUBLANES, FF_CHUNK), F32),
               pltpu.VMEM((2, tm, FF_CHUNK), F32), pltpu.VMEM((2, tm, FF_CHUNK), BF16)]
    return pl.pallas_call(
        functools.partial(_ffn_kernel, n_tiles=nt), grid=(bsz, nt), in_specs=in_specs,
        out_specs=pl.BlockSpec(main, cur), out_shape=jax.ShapeDtypeStruct((bsz, seq, D_MODEL), F32),
        scratch_shapes=scratch,
        compiler_params=pltpu.CompilerParams(dimension_semantics=("parallel", "parallel"),
                                             vmem_limit_bytes=VMEM_LIMIT),
        name="ffn",
    )(h1, h1, h1, *consts)


def _rot_cols(w):
    half = QK_ROPE // 2
    return jnp.concatenate([-w[..., half:], w[..., :half]], axis=-1)


def _block_diag_groups(w):
    per = RNN_GROUP // RNN_BLOCK_W
    w4 = w.reshape(D_RNN // RNN_GROUP, per, RNN_BLOCK_W, RNN_BLOCK_W)
    eye = jnp.eye(per, dtype=w.dtype)
    return jnp.einsum('gkij,kl->gkilj', w4, eye).reshape(D_RNN // RNN_GROUP, RNN_GROUP, RNN_GROUP)


def _prepare_params(ln_in_g, ln_in_b, w_in, rnn_conv_w, rnn_conv_b, lru_w_a, lru_b_a, lru_w_x, lru_b_x, lru_lambda,
                    w_o_rnn, q_norm_g, w_uq, kv_norm_g, w_ukv, w_o_attn, b_gate, w_out, ln1_g, ln1_b, ffn_w_in,
                    ffn_conv_w, ffn_conv_b, ffn_w_out, ln2_g, ln2_b):
    l = 0
    row = lambda v: v.reshape(1, -1).astype(F32)
    w = w_in[l]
    c0, c1, c2, c3 = 2 * D_RNN, 2 * D_RNN + Q_LORA, 2 * D_RNN + Q_LORA + KV_LORA, 2 * D_RNN + Q_LORA + KV_LORA + QK_ROPE
    w_kr = w[:, c2:c3]
    uq = w_uq[l].reshape(Q_LORA, N_HEADS, QK_DIM)
    uq_nope = uq[:, :, :QK_NOPE].reshape(Q_LORA, N_HEADS * QK_NOPE)
    uq_rope = uq[:, :, QK_NOPE:]
    ukv = w_ukv[l].reshape(KV_LORA, N_HEADS, QK_NOPE + V_HEAD)
    n_chunks = D_FF // FF_CHUNK
    fw = ffn_w_in[l]
    chunk_cols = lambda m: m.reshape(m.shape[0], n_chunks, FF_CHUNK).transpose(1, 0, 2)
    return {
        'ln_in_g': row(ln_in_g), 'ln_in_b': row(ln_in_b),
        'w_rnn': w[:, :c0].astype(BF16),
        'w_lat': jnp.concatenate([w[:, c0:c2], w_kr, _rot_cols(w_kr)], axis=1).astype(BF16),
        'w_gate': w[:, c3:].astype(BF16),
        'b_gate': row(b_gate[l]),
        'q_norm_g': row(q_norm_g[l]), 'kv_norm_g': row(kv_norm_g[l]),
        'w_uq': jnp.concatenate([uq_nope, uq_rope.reshape(Q_LORA, -1), _rot_cols(uq_rope).reshape(Q_LORA, -1)],
                                axis=1).astype(BF16),
        'w_uk': ukv[:, :, :QK_NOPE].reshape(KV_LORA, N_HEADS * QK_NOPE).astype(BF16),
        'w_uvt': ukv[:, :, QK_NOPE:].reshape(KV_LORA, N_HEADS * V_HEAD).T.astype(BF16),
        'rnn_conv_w': rnn_conv_w[l].astype(F32), 'rnn_conv_b': row(rnn_conv_b[l]),
        'w_lru': jnp.stack([jnp.concatenate([_block_diag_groups(lru_w_a[l, d]), _block_diag_groups(lru_w_x[l, d])],
                                            axis=2) for d in range(2)]).astype(BF16),
        'lru_b_a': lru_b_a[l].astype(F32), 'lru_b_x': lru_b_x[l].astype(F32), 'lru_lambda': lru_lambda[l].astype(F32),
        'w_o_rnn': w_o_rnn[l].astype(BF16), 'w_o_attn': w_o_attn[l].astype(BF16), 'w_out': w_out[l].astype(BF16),
        'ln1_g': row(ln1_g[l]), 'ln1_b': row(ln1_b[l]),
        'ffn_w_gate': chunk_cols(fw[:, :D_FF]).astype(BF16),
        'ffn_w_up': chunk_cols(fw[:, D_FF:]).astype(BF16),
        'ffn_conv_w': chunk_cols(ffn_conv_w[l]).astype(F32),
        'ffn_conv_b': ffn_conv_b[l].reshape(n_chunks, 1, FF_CHUNK).astype(F32),
        'ffn_w_out': ffn_w_out[l].reshape(n_chunks, FF_CHUNK, D_MODEL).astype(BF16),
        'ln2_g': row(ln2_g[l]), 'ln2_b': row(ln2_b[l]),
    }


def _rope_tables(seq):
    pos = jnp.arange(seq, dtype=F32)
    inv = ROPE_THETA ** (-jnp.arange(0, QK_ROPE, 2, dtype=F32) / QK_ROPE)
    ang = pos[:, None] * inv[None, :]
    reps = LANES // (QK_ROPE // 2)
    return jnp.tile(jnp.cos(ang), (1, reps)), jnp.tile(jnp.sin(ang), (1, reps))


def _trunk(x, p):
    bsz, seq, _ = x.shape
    n = bsz * seq
    x2 = x.reshape(n, D_MODEL)
    cos2, sin2 = _rope_tables(seq)
    pp = dict(p, cos2=cos2, sin2=sin2)
    xr, gg, gate, q, k, v = _proj(x2, seq, pp)
    hf, hb = _rnn(xr.reshape(bsz, seq, D_RNN), p)
    o = _attn(q, k, v, bsz, seq)
    h1 = _mix(x2, hf.reshape(n, D_RNN), hb.reshape(n, D_RNN), gg, o, gate, p)
    return _ffn(h1.reshape(bsz, seq, D_MODEL), p)


def kernel(x_prompt, x_sample, ln_in_g, ln_in_b, w_in, rnn_conv_w, rnn_conv_b, lru_w_a, lru_b_a, lru_w_x, lru_b_x,
           lru_lambda, w_o_rnn, q_norm_g, w_uq, kv_norm_g, w_ukv, w_o_attn, b_gate, w_out, ln1_g, ln1_b, ffn_w_in,
           ffn_conv_w, ffn_conv_b, ffn_w_out, ln2_g, ln2_b):
    p = _prepare_params(ln_in_g, ln_in_b, w_in, rnn_conv_w, rnn_conv_b, lru_w_a, lru_b_a, lru_w_x, lru_b_x,
                        lru_lambda, w_o_rnn, q_norm_g, w_uq, kv_norm_g, w_ukv, w_o_attn, b_gate, w_out, ln1_g,
                        ln1_b, ffn_w_in, ffn_conv_w, ffn_conv_b, ffn_w_out, ln2_g, ln2_b)
    return _trunk(x_prompt, p), _trunk(x_sample, p)
```

```python
import functools

import jax
import jax.numpy as jnp
from jax import lax
from jax.experimental import pallas as pl
from jax.experimental.pallas import tpu as pltpu

D_MODEL = 1024
D_RNN = 1024
RNN_BLOCKS = 16
RNN_BLOCK_W = D_RNN // RNN_BLOCKS
LRU_C = 8.0
N_HEADS = 8
Q_LORA = 384
KV_LORA = 256
QK_NOPE = 128
QK_ROPE = 64
QK_DIM = QK_NOPE + QK_ROPE
V_HEAD = 128
ROPE_THETA = 10000.0
D_FF = 3072
LN_EPS = 1e-5
RMS_EPS = 1e-6
DEPTH = 1
ALPHA = (2.0 * DEPTH) ** 0.25
LOG2_E = 1.4426950408889634
SQRT_FLOOR = 1e-30

SUBLANES = 8
LANES = 128
BF16_ROWS = 16
VMEM_LIMIT = 56 * 1024 * 1024

TM_PROJ = 512
TT_RNN = 512
RNN_GROUP = 256
TQ_ATTN = 1024
TK_ATTN = TM_PROJ
ATTN_ROWS = 32
TM_MIX = 512
TM_FFN = 512
FF_CHUNK = 512

BF16 = jnp.bfloat16
F32 = jnp.float32


def _layer_norm(x, g, b):
    mu = jnp.mean(x, axis=-1, keepdims=True)
    xc = x - mu
    var = jnp.mean(xc * xc, axis=-1, keepdims=True)
    return xc * lax.rsqrt(var + LN_EPS) * g + b


def _rms_norm(x, g):
    return x * lax.rsqrt(jnp.mean(x * x, axis=-1, keepdims=True) + RMS_EPS) * g


def _sigmoid(x):
    return 0.5 * jnp.tanh(0.5 * x) + 0.5


def _dot(a, b):
    return jnp.dot(a, b, preferred_element_type=F32)


def _const_spec(shape):
    nd = len(shape)
    return pl.BlockSpec(shape, lambda *_: (0,) * nd, pipeline_mode=pl.Buffered(1))


def _proj_kernel(x_ref, lng_ref, lnb_ref, w_rnn_ref, w_gate_ref, bgate_ref, w_lat_ref, qg_ref, w_uq_ref,
                 kvg_ref, w_uk_ref, w_uvt_ref, cos_ref, sin_ref,
                 xr_ref, gg_ref, gate_ref, q_ref, k_ref, vt_ref):
    h = _layer_norm(x_ref[...], lng_ref[...], lnb_ref[...]).astype(BF16)

    rnn = _dot(h, w_rnn_ref[...])
    xr_ref[...] = rnn[:, :D_RNN].astype(BF16)
    gg_ref[...] = jax.nn.gelu(rnn[:, D_RNN:], approximate=True).astype(BF16)

    gate_ref[...] = _sigmoid(_dot(h, w_gate_ref[...]) + bgate_ref[...]).astype(BF16)

    cos2 = cos_ref[...]
    sin2 = sin_ref[...]
    cos_q = jnp.concatenate([cos2] * (N_HEADS * QK_ROPE // LANES), axis=1)
    sin_q = jnp.concatenate([sin2] * (N_HEADS * QK_ROPE // LANES), axis=1)

    lat = _dot(h, w_lat_ref[...])
    qn = _rms_norm(lat[:, :Q_LORA], qg_ref[...]).astype(BF16)
    qf = _dot(qn, w_uq_ref[...])
    n_nope = N_HEADS * QK_NOPE
    n_rope = N_HEADS * QK_ROPE
    scale = QK_DIM ** -0.5 * LOG2_E
    q_nope = qf[:, :n_nope] * scale
    q_rope = (qf[:, n_nope:n_nope + n_rope] * cos_q + qf[:, n_nope + n_rope:] * sin_q) * scale

    kvn = _rms_norm(lat[:, Q_LORA:Q_LORA + KV_LORA], kvg_ref[...])
    k_nope = _dot(kvn.astype(BF16), w_uk_ref[...])
    vt = _dot(w_uvt_ref[...], jnp.transpose(kvn).astype(BF16))
    kr = lat[:, Q_LORA + KV_LORA:]
    k_rope = (kr[:, :QK_ROPE] * cos2[:, :QK_ROPE] + kr[:, QK_ROPE:] * sin2[:, :QK_ROPE]).astype(BF16)

    for hd in range(N_HEADS):
        q_ref[hd] = jnp.concatenate(
            [q_nope[:, hd * QK_NOPE:(hd + 1) * QK_NOPE], q_rope[:, hd * QK_ROPE:(hd + 1) * QK_ROPE]],
            axis=1).astype(BF16)
        k_ref[hd] = jnp.concatenate([k_nope[:, hd * QK_NOPE:(hd + 1) * QK_NOPE].astype(BF16), k_rope], axis=1)
        vt_ref[hd, 0] = vt[hd * V_HEAD:(hd + 1) * V_HEAD, :].astype(BF16)


def _proj(x2, seq, p):
    n = x2.shape[0]
    tm = TM_PROJ
    nt_seq = seq // tm
    row = lambda i: (i, 0)
    pos = lambda i: (i % nt_seq, 0)
    head_row = lambda i: (0, i, 0)
    consts = [p['ln_in_g'], p['ln_in_b'], p['w_rnn'], p['w_gate'], p['b_gate'], p['w_lat'], p['q_norm_g'],
              p['w_uq'], p['kv_norm_g'], p['w_uk'], p['w_uvt']]
    in_specs = ([pl.BlockSpec((tm, D_MODEL), row)] + [_const_spec(c.shape) for c in consts]
                + [pl.BlockSpec((tm, LANES), pos), pl.BlockSpec((tm, LANES), pos)])
    out_shape = [
        jax.ShapeDtypeStruct((n, D_RNN), BF16),
        jax.ShapeDtypeStruct((n, D_RNN), BF16),
        jax.ShapeDtypeStruct((n, 2 * D_MODEL), BF16),
        jax.ShapeDtypeStruct((N_HEADS, n, QK_DIM), BF16),
        jax.ShapeDtypeStruct((N_HEADS, n, QK_DIM), BF16),
        jax.ShapeDtypeStruct((N_HEADS, n // tm, V_HEAD, tm), BF16),
    ]
    out_specs = [
        pl.BlockSpec((tm, D_RNN), row), pl.BlockSpec((tm, D_RNN), row), pl.BlockSpec((tm, 2 * D_MODEL), row),
        pl.BlockSpec((N_HEADS, tm, QK_DIM), head_row), pl.BlockSpec((N_HEADS, tm, QK_DIM), head_row),
        pl.BlockSpec((N_HEADS, 1, V_HEAD, tm), lambda i: (0, i, 0, 0)),
    ]
    return pl.pallas_call(
        _proj_kernel, grid=(n // tm,), in_specs=in_specs, out_specs=out_specs, out_shape=out_shape,
        compiler_params=pltpu.CompilerParams(dimension_semantics=("parallel",), vmem_limit_bytes=VMEM_LIMIT),
        name="proj",
    )(x2, *consts, p['cos2'], p['sin2'])


def _rnn_kernel(xf_ref, xfp_ref, xfn_ref, xb_ref, xbp_ref, xbn_ref, perm_ref, inv_ref, cw_ref, cb_ref, wg_ref,
                ba_ref, bx_ref, lam_ref, hf_ref, hb_ref, xs_ref, a_ref, b_ref, carry_ref, *, n_tiles):
    j = pl.program_id(1)
    tt = TT_RNN
    n_l = tt // SUBLANES
    top = SUBLANES - 1

    @pl.when(j == 0)
    def _():
        carry_ref[...] = jnp.zeros_like(carry_ref)

    sub = lax.broadcasted_iota(jnp.int32, (SUBLANES, D_RNN), 0)

    def halo_row(blk, r, valid):
        row = jnp.broadcast_to(blk[r:r + 1, :], (SUBLANES, D_RNN))
        return jnp.where(valid, row, 0.0)

    def one_direction(d, x_ref, xp_ref, xn_ref, tile, out_ref):
        prev = xp_ref[0].astype(F32)
        nxt = xn_ref[0].astype(F32)
        has_prev = tile > 0
        has_next = tile < n_tiles - 1
        xs_ref[2:2 + n_l] = _dot(perm_ref[d], x_ref[0]).reshape(n_l, SUBLANES, D_RNN)

        def before(l, fill):
            return jnp.where(sub >= 1, pltpu.roll(xs_ref[2 + l], 1, 0), fill)

        def after(l, fill):
            return jnp.where(sub < top, pltpu.roll(xs_ref[2 + l], top, 0), fill)

        if d == 0:
            xs_ref[1] = before(n_l - 1, halo_row(prev, BF16_ROWS - 1, has_prev))
            xs_ref[0] = before(n_l - 2, halo_row(prev, BF16_ROWS - 2, has_prev))
            xs_ref[2 + n_l] = after(0, halo_row(nxt, 0, has_next))
            offsets = (-2, -1, 0, 1)
        else:
            xs_ref[1] = before(n_l - 1, halo_row(nxt, 0, has_next))
            xs_ref[2 + n_l] = after(0, halo_row(prev, BF16_ROWS - 1, has_prev))
            xs_ref[3 + n_l] = after(1, halo_row(prev, BF16_ROWS - 2, has_prev))
            offsets = (2, 1, 0, -1)
        u = cb_ref[...] + cw_ref[0:1, :] * xs_ref[2 + offsets[0]:2 + offsets[0] + n_l]
        for k in range(1, 4):
            u = u + cw_ref[k:k + 1, :] * xs_ref[2 + offsets[k]:2 + offsets[k] + n_l]
        u = u.reshape(tt, D_RNN)

        sp = jax.nn.softplus(-lam_ref[d:d + 1, :])
        for g in range(D_RNN // RNN_GROUP):
            cs = slice(g * RNN_GROUP, (g + 1) * RNN_GROUP)
            ug = u[:, cs]
            z = _dot(ug.astype(BF16), wg_ref[d, g])
            r = _sigmoid(z[:, :RNN_GROUP] + ba_ref[d:d + 1, cs])
            i = _sigmoid(z[:, RNN_GROUP:] + bx_ref[d:d + 1, cs])
            log_a = -LRU_C * r * sp[:, cs]
            a = jnp.exp(log_a)
            a_ref[:, cs] = a
            y = 1.0 - a * a
            b_ref[:, cs] = y * lax.rsqrt(jnp.maximum(y, SQRT_FLOOR)) * (i * ug)

        def body(l, carry):
            hv, pv = carry
            r0 = pl.multiple_of(l * SUBLANES, SUBLANES)
            av = a_ref[pl.ds(r0, SUBLANES), :]
            hv = av * hv + b_ref[pl.ds(r0, SUBLANES), :]
            pv = av * pv
            b_ref[pl.ds(r0, SUBLANES), :] = hv
            a_ref[pl.ds(r0, SUBLANES), :] = pv
            return hv, pv

        zeros = jnp.zeros((SUBLANES, D_RNN), F32)
        bv, av = lax.fori_loop(0, n_l, body, (zeros, zeros + 1.0), unroll=4)
        for s in (1, 2, 4):
            m = sub >= s
            ar = jnp.where(m, pltpu.roll(av, s, 0), 1.0)
            br = jnp.where(m, pltpu.roll(bv, s, 0), 0.0)
            bv = av * br + bv
            av = av * ar
        entry = carry_ref[d]
        ends = av * entry + bv
        chunk_entry = jnp.where(sub >= 1, pltpu.roll(ends, 1, 0), entry)
        carry_ref[d] = jnp.broadcast_to(ends[top:top + 1, :], (SUBLANES, D_RNN))
        hs = (b_ref[...].reshape(n_l, SUBLANES, D_RNN)
              + a_ref[...].reshape(n_l, SUBLANES, D_RNN) * chunk_entry).reshape(tt, D_RNN)
        out_ref[0] = _dot(inv_ref[d], hs.astype(BF16)).astype(BF16)

    one_direction(0, xf_ref, xfp_ref, xfn_ref, j, hf_ref)
    one_direction(1, xb_ref, xbp_ref, xbn_ref, n_tiles - 1 - j, hb_ref)


def _scan_permutations(tt):
    row = jnp.arange(tt)
    step = (row % SUBLANES) * (tt // SUBLANES) + row // SUBLANES
    src = jnp.stack([step, tt - 1 - step])
    perm = (src[:, :, None] == row[None, None, :]).astype(BF16)
    return perm, perm.transpose(0, 2, 1)


def _rnn(xr3, p):
    bsz, seq, _ = xr3.shape
    tt = TT_RNN
    nt = seq // tt
    hpt = tt // BF16_ROWS
    n_halo = seq // BF16_ROWS
    fwd = lambda b, j: (b, j, 0)
    fwd_p = lambda b, j: (b, jnp.maximum(j * hpt - 1, 0), 0)
    fwd_n = lambda b, j: (b, jnp.minimum((j + 1) * hpt, n_halo - 1), 0)
    bwd = lambda b, j: (b, nt - 1 - j, 0)
    bwd_p = lambda b, j: (b, jnp.maximum((nt - 1 - j) * hpt - 1, 0), 0)
    bwd_n = lambda b, j: (b, jnp.minimum((nt - j) * hpt, n_halo - 1), 0)
    perm, inv = _scan_permutations(tt)
    consts = [perm, inv, p['rnn_conv_w'], p['rnn_conv_b'], p['w_lru'], p['lru_b_a'], p['lru_b_x'], p['lru_lambda']]
    main = (1, tt, D_RNN)
    hal = (1, BF16_ROWS, D_RNN)
    in_specs = [pl.BlockSpec(main, fwd), pl.BlockSpec(hal, fwd_p), pl.BlockSpec(hal, fwd_n),
                pl.BlockSpec(main, bwd), pl.BlockSpec(hal, bwd_p), pl.BlockSpec(hal, bwd_n)]
    in_specs += [_const_spec(c.shape) for c in consts]
    out_shape = [jax.ShapeDtypeStruct((bsz, seq, D_RNN), BF16)] * 2
    out_specs = [pl.BlockSpec(main, fwd), pl.BlockSpec(main, bwd)]
    scratch = [pltpu.VMEM((tt // SUBLANES + 4, SUBLANES, D_RNN), F32), pltpu.VMEM((tt, D_RNN), F32),
               pltpu.VMEM((tt, D_RNN), F32), pltpu.VMEM((2, SUBLANES, D_RNN), F32)]
    return pl.pallas_call(
        functools.partial(_rnn_kernel, n_tiles=nt), grid=(bsz, nt), in_specs=in_specs, out_specs=out_specs,
        out_shape=out_shape, scratch_shapes=scratch,
        compiler_params=pltpu.CompilerParams(dimension_semantics=("parallel", "arbitrary"),
                                             vmem_limit_bytes=VMEM_LIMIT),
        name="rnn",
    )(xr3, xr3, xr3, xr3, xr3, xr3, *consts)


def _attn_kernel(q_ref, k_ref, vt_ref, o_ref, s_ref, p_ref, acc_ref, *, n_kv):
    tk = TK_ATTN
    qt = jnp.transpose(q_ref[0].astype(F32)).astype(BF16)

    def scores(i, slot):
        r0 = pl.multiple_of(i * tk, tk)
        s = _dot(k_ref[0, pl.ds(r0, tk), :], qt)
        s_ref[slot] = s
        return jnp.max(s, axis=0, keepdims=True)

    def exponentials(slot, m, l, cmax):
        m_new = jnp.maximum(m, cmax)
        alpha = jnp.exp2(m - m_new)
        m_rows = jnp.broadcast_to(m_new, (ATTN_ROWS, tq))
        part = jnp.zeros((SUBLANES, tq), F32)
        for r in range(0, tk, ATTN_ROWS):
            pr = jnp.exp2(s_ref[slot, r:r + ATTN_ROWS, :] - m_rows)
            p_ref[slot, r:r + ATTN_ROWS, :] = pr.astype(BF16)
            part = part + jnp.sum(pr.reshape(ATTN_ROWS // SUBLANES, SUBLANES, tq), axis=0)
        return m_new, alpha * l + jnp.sum(part, axis=0, keepdims=True), alpha

    def values(i, slot, alpha):
        acc_ref[...] = alpha * acc_ref[...] + _dot(vt_ref[0, i], p_ref[slot])

    tq = qt.shape[1]
    m = jnp.full((1, tq), -jnp.inf, F32)
    l = jnp.zeros((1, tq), F32)
    acc_ref[...] = jnp.zeros_like(acc_ref)
    cmax = scores(0, 0)
    cmax_next = scores(1, 1)
    m, l, alpha = exponentials(0, m, l, cmax)
    cmax = cmax_next

    alpha_prev = alpha
    for i in range(1, n_kv - 1):
        cur = i % 2
        cmax_next = scores(i + 1, 1 - cur)
        m, l, alpha = exponentials(cur, m, l, cmax)
        values(i - 1, 1 - cur, alpha_prev)
        cmax, alpha_prev = cmax_next, alpha
    last = n_kv - 1
    m, l, alpha = exponentials(last % 2, m, l, cmax)
    values(last - 1, 1 - last % 2, alpha_prev)
    values(last, last % 2, alpha)
    o_ref[...] = jnp.transpose(acc_ref[...] / l).astype(BF16)


def _attn(q, k, vt, bsz, seq):
    n = bsz * seq
    tq = TQ_ATTN
    nq = seq // tq
    n_kv = seq // TK_ATTN
    in_specs = [pl.BlockSpec((1, tq, QK_DIM), lambda b, h, i: (h, b * nq + i, 0)),
                pl.BlockSpec((1, seq, QK_DIM), lambda b, h, i: (h, b, 0)),
                pl.BlockSpec((1, n_kv, V_HEAD, TK_ATTN), lambda b, h, i: (h, b, 0, 0))]
    out_specs = pl.BlockSpec((tq, V_HEAD), lambda b, h, i: (b * nq + i, h))
    assert n_kv % 2 == 0 and n_kv >= 4
    scratch = [pltpu.VMEM((2, TK_ATTN, tq), F32), pltpu.VMEM((2, TK_ATTN, tq), BF16),
               pltpu.VMEM((V_HEAD, tq), F32)]
    return pl.pallas_call(
        functools.partial(_attn_kernel, n_kv=n_kv), grid=(bsz, N_HEADS, nq), in_specs=in_specs,
        out_specs=out_specs, out_shape=jax.ShapeDtypeStruct((n, N_HEADS * V_HEAD), BF16),
        scratch_shapes=scratch,
        compiler_params=pltpu.CompilerParams(dimension_semantics=("parallel", "parallel", "arbitrary"),
                                             vmem_limit_bytes=VMEM_LIMIT),
        name="attn",
    )(q, k, vt)


def _mix_kernel(x_ref, hf_ref, hb_ref, gg_ref, o_ref, gate_ref, lng_ref, lnb_ref, w_or_ref, w_oa_ref, w_out_ref,
                l1g_ref, l1b_ref, h1_ref):
    h = _layer_norm(x_ref[...], lng_ref[...], lnb_ref[...])
    rnn_in = (hf_ref[...].astype(F32) + hb_ref[...].astype(F32)) * gg_ref[...].astype(F32)
    y_rnn = _dot(rnn_in.astype(BF16), w_or_ref[...])
    y_attn = _dot(o_ref[...], w_oa_ref[...])
    gates = gate_ref[...].astype(F32)
    merged = gates[:, :D_MODEL] * y_rnn + gates[:, D_MODEL:] * y_attn
    mix = _dot(merged.astype(BF16), w_out_ref[...])
    h1_ref[...] = _layer_norm(ALPHA * h + mix, l1g_ref[...], l1b_ref[...])


def _mix(x2, hf, hb, gg, o, gate, p):
    n = x2.shape[0]
    tm = TM_MIX
    row = lambda i: (i, 0)
    acts = [x2, hf, hb, gg, o, gate]
    consts = [p['ln_in_g'], p['ln_in_b'], p['w_o_rnn'], p['w_o_attn'], p['w_out'], p['ln1_g'], p['ln1_b']]
    in_specs = [pl.BlockSpec((tm, a.shape[1]), row) for a in acts] + [_const_spec(c.shape) for c in consts]
    return pl.pallas_call(
        _mix_kernel, grid=(n // tm,), in_specs=in_specs, out_specs=pl.BlockSpec((tm, D_MODEL), row),
        out_shape=jax.ShapeDtypeStruct((n, D_MODEL), F32),
        compiler_params=pltpu.CompilerParams(dimension_semantics=("parallel",), vmem_limit_bytes=VMEM_LIMIT),
        name="mix",
    )(*acts, *consts)


def _ffn_kernel(h_ref, hp_ref, hn_ref, wg_ref, wu_ref, cw_ref, cb_ref, wo_ref, l2g_ref, l2b_ref, out_ref,
                ext_ref, g_ref, u_ref, a_ref, *, n_tiles):
    j = pl.program_id(1)
    tm = TM_FFN
    halo = SUBLANES
    hm = h_ref[0]
    ext_ref[0:halo, :] = jnp.where(j > 0, hp_ref[0], 0.0)
    ext_ref[halo:halo + tm, :] = hm
    ext_ref[halo + tm:2 * halo + tm, :] = jnp.where(j < n_tiles - 1, hn_ref[0], 0.0)
    ext = ext_ref[...].astype(BF16)
    hmb = hm.astype(BF16)

    n_chunks = D_FF // FF_CHUNK

    def up_project(c):
        g_ref[c % 2] = _dot(ext, wg_ref[c])
        u_ref[c % 2] = _dot(hmb, wu_ref[c])

    def activate(c):
        g = g_ref.at[c % 2]
        cw = cw_ref[c]
        gc = cb_ref[c] + cw[0:1, :] * g[halo - 1:halo - 1 + tm, :]
        gc = gc + cw[1:2, :] * g[halo:halo + tm, :]
        gc = gc + cw[2:3, :] * g[halo + 1:halo + 1 + tm, :]
        a_ref[c % 2] = (jax.nn.gelu(gc, approximate=True) * u_ref[c % 2]).astype(BF16)

    def down_project(c, acc):
        part = _dot(a_ref[c % 2], wo_ref[c])
        return part if acc is None else acc + part

    acc = None
    up_project(0)
    for c in range(n_chunks):
        if c + 1 < n_chunks:
            up_project(c + 1)
        activate(c)
        if c >= 1:
            acc = down_project(c - 1, acc)
    acc = down_project(n_chunks - 1, acc)
    out_ref[0] = _layer_norm(ALPHA * hm + acc, l2g_ref[...], l2b_ref[...])


def _ffn(h1, p):
    bsz, seq, _ = h1.shape
    tm = TM_FFN
    nt = seq // tm
    hpt = tm // SUBLANES
    n_halo = seq // SUBLANES
    main = (1, tm, D_MODEL)
    hal = (1, SUBLANES, D_MODEL)
    cur = lambda b, j: (b, j, 0)
    prv = lambda b, j: (b, jnp.maximum(j * hpt - 1, 0), 0)
    nxt = lambda b, j: (b, jnp.minimum((j + 1) * hpt, n_halo - 1), 0)
    consts = [p['ffn_w_gate'], p['ffn_w_up'], p['ffn_conv_w'], p['ffn_conv_b'], p['ffn_w_out'], p['ln2_g'],
              p['ln2_b']]
    in_specs = [pl.BlockSpec(main, cur), pl.BlockSpec(hal, prv), pl.BlockSpec(hal, nxt)]
    in_specs += [_const_spec(c.shape) for c in consts]
    scratch = [pltpu.VMEM((tm + 2 * SUBLANES, D_MODEL), F32), pltpu.VMEM((2, tm + 2 * SUBLANES, FF_CHUNK), F32),
               pltpu.VMEM((2, tm, FF_CHUNK), F32), pltpu.VMEM((2, tm, FF_CHUNK), BF16)]
    return pl.pallas_call(
        functools.partial(_ffn_kernel, n_tiles=nt), grid=(bsz, nt), in_specs=in_specs,
        out_specs=pl.BlockSpec(main, cur), out_shape=jax.ShapeDtypeStruct((bsz, seq, D_MODEL), F32),
        scratch_shapes=scratch,
        compiler_params=pltpu.CompilerParams(dimension_semantics=("parallel", "parallel"),
                                             vmem_limit_bytes=VMEM_LIMIT),
        name="ffn",
    )(h1, h1, h1, *consts)


def _rot_cols(w):
    half = QK_ROPE // 2
    return jnp.concatenate([-w[..., half:], w[..., :half]], axis=-1)


def _block_diag_groups(w):
    per = RNN_GROUP // RNN_BLOCK_W
    w4 = w.reshape(D_RNN // RNN_GROUP, per, RNN_BLOCK_W, RNN_BLOCK_W)
    eye = jnp.eye(per, dtype=w.dtype)
    return jnp.einsum('gkij,kl->gkilj', w4, eye).reshape(D_RNN // RNN_GROUP, RNN_GROUP, RNN_GROUP)


def _prepare_params(ln_in_g, ln_in_b, w_in, rnn_conv_w, rnn_conv_b, lru_w_a, lru_b_a, lru_w_x, lru_b_x, lru_lambda,
                    w_o_rnn, q_norm_g, w_uq, kv_norm_g, w_ukv, w_o_attn, b_gate, w_out, ln1_g, ln1_b, ffn_w_in,
                    ffn_conv_w, ffn_conv_b, ffn_w_out, ln2_g, ln2_b):
    l = 0
    row = lambda v: v.reshape(1, -1).astype(F32)
    w = w_in[l]
    c0, c1, c2, c3 = 2 * D_RNN, 2 * D_RNN + Q_LORA, 2 * D_RNN + Q_LORA + KV_LORA, 2 * D_RNN + Q_LORA + KV_LORA + QK_ROPE
    w_kr = w[:, c2:c3]
    uq = w_uq[l].reshape(Q_LORA, N_HEADS, QK_DIM)
    uq_nope = uq[:, :, :QK_NOPE].reshape(Q_LORA, N_HEADS * QK_NOPE)
    uq_rope = uq[:, :, QK_NOPE:]
    ukv = w_ukv[l].reshape(KV_LORA, N_HEADS, QK_NOPE + V_HEAD)
    n_chunks = D_FF // FF_CHUNK
    fw = ffn_w_in[l]
    chunk_cols = lambda m: m.reshape(m.shape[0], n_chunks, FF_CHUNK).transpose(1, 0, 2)
    return {
        'ln_in_g': row(ln_in_g), 'ln_in_b': row(ln_in_b),
        'w_rnn': w[:, :c0].astype(BF16),
        'w_lat': jnp.concatenate([w[:, c0:c2], w_kr, _rot_cols(w_kr)], axis=1).astype(BF16),
        'w_gate': w[:, c3:].astype(BF16),
        'b_gate': row(b_gate[l]),
        'q_norm_g': row(q_norm_g[l]), 'kv_norm_g': row(kv_norm_g[l]),
        'w_uq': jnp.concatenate([uq_nope, uq_rope.reshape(Q_LORA, -1), _rot_cols(uq_rope).reshape(Q_LORA, -1)],
                                axis=1).astype(BF16),
        'w_uk': ukv[:, :, :QK_NOPE].reshape(KV_LORA, N_HEADS * QK_NOPE).astype(BF16),
        'w_uvt': ukv[:, :, QK_NOPE:].reshape(KV_LORA, N_HEADS * V_HEAD).T.astype(BF16),
        'rnn_conv_w': rnn_conv_w[l].astype(F32), 'rnn_conv_b': row(rnn_conv_b[l]),
        'w_lru': jnp.stack([jnp.concatenate([_block_diag_groups(lru_w_a[l, d]), _block_diag_groups(lru_w_x[l, d])],
                                            axis=2) for d in range(2)]).astype(BF16),
        'lru_b_a': lru_b_a[l].astype(F32), 'lru_b_x': lru_b_x[l].astype(F32), 'lru_lambda': lru_lambda[l].astype(F32),
        'w_o_rnn': w_o_rnn[l].astype(BF16), 'w_o_attn': w_o_attn[l].astype(BF16), 'w_out': w_out[l].astype(BF16),
        'ln1_g': row(ln1_g[l]), 'ln1_b': row(ln1_b[l]),
        'ffn_w_gate': chunk_cols(fw[:, :D_FF]).astype(BF16),
        'ffn_w_up': chunk_cols(fw[:, D_FF:]).astype(BF16),
        'ffn_conv_w': chunk_cols(ffn_conv_w[l]).astype(F32),
        'ffn_conv_b': ffn_conv_b[l].reshape(n_chunks, 1, FF_CHUNK).astype(F32),
        'ffn_w_out': ffn_w_out[l].reshape(n_chunks, FF_CHUNK, D_MODEL).astype(BF16),
        'ln2_g': row(ln2_g[l]), 'ln2_b': row(ln2_b[l]),
    }


def _rope_tables(seq):
    pos = jnp.arange(seq, dtype=F32)
    inv = ROPE_THETA ** (-jnp.arange(0, QK_ROPE, 2, dtype=F32) / QK_ROPE)
    ang = pos[:, None] * inv[None, :]
    reps = LANES // (QK_ROPE // 2)
    return jnp.tile(jnp.cos(ang), (1, reps)), jnp.tile(jnp.sin(ang), (1, reps))


def _trunk(x, p):
    bsz, seq, _ = x.shape
    n = bsz * seq
    x2 = x.reshape(n, D_MODEL)
    cos2, sin2 = _rope_tables(seq)
    pp = dict(p, cos2=cos2, sin2=sin2)
    xr, gg, gate, q, k, v = _proj(x2, seq, pp)
    hf, hb = _rnn(xr.reshape(bsz, seq, D_RNN), p)
    o = _attn(q, k, v, bsz, seq)
    h1 = _mix(x2, hf.reshape(n, D_RNN), hb.reshape(n, D_RNN), gg, o, gate, p)
    return _ffn(h1.reshape(bsz, seq, D_MODEL), p)


def kernel(x_prompt, x_sample, ln_in_g, ln_in_b, w_in, rnn_conv_w, rnn_conv_b, lru_w_a, lru_b_a, lru_w_x, lru_b_x,
           lru_lambda, w_o_rnn, q_norm_g, w_uq, kv_norm_g, w_ukv, w_o_attn, b_gate, w_out, ln1_g, ln1_b, ffn_w_in,
           ffn_conv_w, ffn_conv_b, ffn_w_out, ln2_g, ln2_b):
    p = _prepare_params(ln_in_g, ln_in_b, w_in, rnn_conv_w, rnn_conv_b, lru_w_a, lru_b_a, lru_w_x, lru_b_x,
                        lru_lambda, w_o_rnn, q_norm_g, w_uq, kv_norm_g, w_ukv, w_o_attn, b_gate, w_out, ln1_g,
                        ln1_b, ffn_w_in, ffn_conv_w, ffn_conv_b, ffn_w_out, ln2_g, ln2_b)
    return _trunk(x_prompt, p), _trunk(x_sample, p)
```

```python
import functools

import jax
import jax.numpy as jnp
from jax import lax
from jax.experimental import pallas as pl
from jax.experimental.pallas import tpu as pltpu

D_MODEL = 1024
D_RNN = 1024
RNN_BLOCKS = 16
RNN_BLOCK_W = D_RNN // RNN_BLOCKS
LRU_C = 8.0
N_HEADS = 8
Q_LORA = 384
KV_LORA = 256
QK_NOPE = 128
QK_ROPE = 64
QK_DIM = QK_NOPE + QK_ROPE
V_HEAD = 128
ROPE_THETA = 10000.0
D_FF = 3072
LN_EPS = 1e-5
RMS_EPS = 1e-6
DEPTH = 1
ALPHA = (2.0 * DEPTH) ** 0.25
LOG2_E = 1.4426950408889634
SQRT_FLOOR = 1e-30

SUBLANES = 8
LANES = 128
BF16_ROWS = 16
VMEM_LIMIT = 56 * 1024 * 1024

TM_PROJ = 512
TT_RNN = 512
RNN_GROUP = 256
TQ_ATTN = 1024
TK_ATTN = TM_PROJ
ATTN_ROWS = 32
TM_MIX = 512
TM_FFN = 512
FF_CHUNK = 1024

BF16 = jnp.bfloat16
F32 = jnp.float32


def _layer_norm(x, g, b):
    mu = jnp.mean(x, axis=-1, keepdims=True)
    xc = x - mu
    var = jnp.mean(xc * xc, axis=-1, keepdims=True)
    return xc * lax.rsqrt(var + LN_EPS) * g + b


def _rms_norm(x, g):
    return x * lax.rsqrt(jnp.mean(x * x, axis=-1, keepdims=True) + RMS_EPS) * g


def _sigmoid(x):
    return 0.5 * jnp.tanh(0.5 * x) + 0.5


def _dot(a, b):
    return jnp.dot(a, b, preferred_element_type=F32)


def _const_spec(shape):
    nd = len(shape)
    return pl.BlockSpec(shape, lambda *_: (0,) * nd, pipeline_mode=pl.Buffered(1))


def _proj_kernel(x_ref, lng_ref, lnb_ref, w_rnn_ref, w_gate_ref, bgate_ref, w_lat_ref, qg_ref, w_uq_ref,
                 kvg_ref, w_uk_ref, w_uvt_ref, cos_ref, sin_ref,
                 xr_ref, gg_ref, gate_ref, q_ref, k_ref, vt_ref):
    h = _layer_norm(x_ref[...], lng_ref[...], lnb_ref[...]).astype(BF16)

    rnn = _dot(h, w_rnn_ref[...])
    xr_ref[...] = rnn[:, :D_RNN].astype(BF16)
    gg_ref[...] = jax.nn.gelu(rnn[:, D_RNN:], approximate=True).astype(BF16)

    gate_ref[...] = _sigmoid(_dot(h, w_gate_ref[...]) + bgate_ref[...]).astype(BF16)

    cos2 = cos_ref[...]
    sin2 = sin_ref[...]
    cos_q = jnp.concatenate([cos2] * (N_HEADS * QK_ROPE // LANES), axis=1)
    sin_q = jnp.concatenate([sin2] * (N_HEADS * QK_ROPE // LANES), axis=1)

    lat = _dot(h, w_lat_ref[...])
    qn = _rms_norm(lat[:, :Q_LORA], qg_ref[...]).astype(BF16)
    qf = _dot(qn, w_uq_ref[...])
    n_nope = N_HEADS * QK_NOPE
    n_rope = N_HEADS * QK_ROPE
    scale = QK_DIM ** -0.5 * LOG2_E
    q_nope = qf[:, :n_nope] * scale
    q_rope = (qf[:, n_nope:n_nope + n_rope] * cos_q + qf[:, n_nope + n_rope:] * sin_q) * scale

    kvn = _rms_norm(lat[:, Q_LORA:Q_LORA + KV_LORA], kvg_ref[...])
    k_nope = _dot(kvn.astype(BF16), w_uk_ref[...])
    vt = _dot(w_uvt_ref[...], jnp.transpose(kvn).astype(BF16))
    kr = lat[:, Q_LORA + KV_LORA:]
    k_rope = (kr[:, :QK_ROPE] * cos2[:, :QK_ROPE] + kr[:, QK_ROPE:] * sin2[:, :QK_ROPE]).astype(BF16)

    for hd in range(N_HEADS):
        q_ref[hd] = jnp.concatenate(
            [q_nope[:, hd * QK_NOPE:(hd + 1) * QK_NOPE], q_rope[:, hd * QK_ROPE:(hd + 1) * QK_ROPE]],
            axis=1).astype(BF16)
        k_ref[hd] = jnp.concatenate([k_nope[:, hd * QK_NOPE:(hd + 1) * QK_NOPE].astype(BF16), k_rope], axis=1)
        vt_ref[hd, 0] = vt[hd * V_HEAD:(hd + 1) * V_HEAD, :].astype(BF16)


def _proj(x2, seq, p):
    n = x2.shape[0]
    tm = TM_PROJ
    nt_seq = seq // tm
    row = lambda i: (i, 0)
    pos = lambda i: (i % nt_seq, 0)
    head_row = lambda i: (0, i, 0)
    consts = [p['ln_in_g'], p['ln_in_b'], p['w_rnn'], p['w_gate'], p['b_gate'], p['w_lat'], p['q_norm_g'],
              p['w_uq'], p['kv_norm_g'], p['w_uk'], p['w_uvt']]
    in_specs = ([pl.BlockSpec((tm, D_MODEL), row)] + [_const_spec(c.shape) for c in consts]
                + [pl.BlockSpec((tm, LANES), pos), pl.BlockSpec((tm, LANES), pos)])
    out_shape = [
        jax.ShapeDtypeStruct((n, D_RNN), BF16),
        jax.ShapeDtypeStruct((n, D_RNN), BF16),
        jax.ShapeDtypeStruct((n, 2 * D_MODEL), BF16),
        jax.ShapeDtypeStruct((N_HEADS, n, QK_DIM), BF16),
        jax.ShapeDtypeStruct((N_HEADS, n, QK_DIM), BF16),
        jax.ShapeDtypeStruct((N_HEADS, n // tm, V_HEAD, tm), BF16),
    ]
    out_specs = [
        pl.BlockSpec((tm, D_RNN), row), pl.BlockSpec((tm, D_RNN), row), pl.BlockSpec((tm, 2 * D_MODEL), row),
        pl.BlockSpec((N_HEADS, tm, QK_DIM), head_row), pl.BlockSpec((N_HEADS, tm, QK_DIM), head_row),
        pl.BlockSpec((N_HEADS, 1, V_HEAD, tm), lambda i: (0, i, 0, 0)),
    ]
    return pl.pallas_call(
        _proj_kernel, grid=(n // tm,), in_specs=in_specs, out_specs=out_specs, out_shape=out_shape,
        compiler_params=pltpu.CompilerParams(dimension_semantics=("parallel",), vmem_limit_bytes=VMEM_LIMIT),
        name="proj",
    )(x2, *consts, p['cos2'], p['sin2'])


def _rnn_kernel(xf_ref, xfp_ref, xfn_ref, xb_ref, xbp_ref, xbn_ref, perm_ref, inv_ref, cw_ref, cb_ref, wg_ref,
                ba_ref, bx_ref, lam_ref, hf_ref, hb_ref, xs_ref, a_ref, b_ref, carry_ref, *, n_tiles):
    j = pl.program_id(1)
    tt = TT_RNN
    n_l = tt // SUBLANES
    top = SUBLANES - 1

    @pl.when(j == 0)
    def _():
        carry_ref[...] = jnp.zeros_like(carry_ref)

    sub = lax.broadcasted_iota(jnp.int32, (SUBLANES, D_RNN), 0)

    def halo_row(blk, r, valid):
        row = jnp.broadcast_to(blk[r:r + 1, :], (SUBLANES, D_RNN))
        return jnp.where(valid, row, 0.0)

    def coefficients(d, x_ref, xp_ref, xn_ref, tile):
        xs = xs_ref.at[d]
        prev = xp_ref[0].astype(F32)
        nxt = xn_ref[0].astype(F32)
        has_prev = tile > 0
        has_next = tile < n_tiles - 1
        xs[2:2 + n_l] = _dot(perm_ref[d], x_ref[0]).reshape(n_l, SUBLANES, D_RNN)

        def before(l, fill):
            return jnp.where(sub >= 1, pltpu.roll(xs[2 + l], 1, 0), fill)

        def after(l, fill):
            return jnp.where(sub < top, pltpu.roll(xs[2 + l], top, 0), fill)

        if d == 0:
            xs[1] = before(n_l - 1, halo_row(prev, BF16_ROWS - 1, has_prev))
            xs[0] = before(n_l - 2, halo_row(prev, BF16_ROWS - 2, has_prev))
            xs[2 + n_l] = after(0, halo_row(nxt, 0, has_next))
            offsets = (-2, -1, 0, 1)
        else:
            xs[1] = before(n_l - 1, halo_row(nxt, 0, has_next))
            xs[2 + n_l] = after(0, halo_row(prev, BF16_ROWS - 1, has_prev))
            xs[3 + n_l] = after(1, halo_row(prev, BF16_ROWS - 2, has_prev))
            offsets = (2, 1, 0, -1)
        u = cb_ref[...] + cw_ref[0:1, :] * xs[2 + offsets[0]:2 + offsets[0] + n_l]
        for k in range(1, 4):
            u = u + cw_ref[k:k + 1, :] * xs[2 + offsets[k]:2 + offsets[k] + n_l]
        u = u.reshape(tt, D_RNN)

        sp = jax.nn.softplus(-lam_ref[d:d + 1, :])
        for g in range(D_RNN // RNN_GROUP):
            cs = slice(g * RNN_GROUP, (g + 1) * RNN_GROUP)
            ug = u[:, cs]
            z = _dot(ug.astype(BF16), wg_ref[d, g])
            tr = jnp.tanh(z[:, :RNN_GROUP] + ba_ref[d:d + 1, cs])
            i = 0.5 * jnp.tanh(z[:, RNN_GROUP:] + bx_ref[d:d + 1, cs]) + 0.5
            half_c = (-0.5 * LRU_C) * sp[:, cs]
            log_a = half_c * tr + half_c
            a = jnp.exp(log_a)
            a_ref[d, :, cs] = a
            y = 1.0 - a * a
            b_ref[d, :, cs] = y * lax.rsqrt(jnp.maximum(y, SQRT_FLOOR)) * (i * ug)

    def body(l, carry):
        r0 = pl.multiple_of(l * SUBLANES, SUBLANES)
        out = []
        for d in range(2):
            hv, pv = carry[d]
            av = a_ref[d, pl.ds(r0, SUBLANES), :]
            hv = av * hv + b_ref[d, pl.ds(r0, SUBLANES), :]
            pv = av * pv
            b_ref[d, pl.ds(r0, SUBLANES), :] = hv
            a_ref[d, pl.ds(r0, SUBLANES), :] = pv
            out.append((hv, pv))
        return tuple(out)

    def finish(d, bv, av, out_ref):
        for s in (1, 2, 4):
            m = sub >= s
            ar = jnp.where(m, pltpu.roll(av, s, 0), 1.0)
            br = jnp.where(m, pltpu.roll(bv, s, 0), 0.0)
            bv = av * br + bv
            av = av * ar
        entry = carry_ref[d]
        ends = av * entry + bv
        chunk_entry = jnp.where(sub >= 1, pltpu.roll(ends, 1, 0), entry)
        carry_ref[d] = jnp.broadcast_to(ends[top:top + 1, :], (SUBLANES, D_RNN))
        hs = (b_ref[d].reshape(n_l, SUBLANES, D_RNN)
              + a_ref[d].reshape(n_l, SUBLANES, D_RNN) * chunk_entry).reshape(tt, D_RNN)
        out_ref[0] = _dot(inv_ref[d], hs.astype(BF16)).astype(BF16)

    coefficients(0, xf_ref, xfp_ref, xfn_ref, j)
    coefficients(1, xb_ref, xbp_ref, xbn_ref, n_tiles - 1 - j)
    zeros = jnp.zeros((SUBLANES, D_RNN), F32)
    start = (zeros, zeros + 1.0)
    (bv0, av0), (bv1, av1) = lax.fori_loop(0, n_l, body, (start, start), unroll=4)
    finish(0, bv0, av0, hf_ref)
    finish(1, bv1, av1, hb_ref)


def _scan_permutations(tt):
    row = jnp.arange(tt)
    step = (row % SUBLANES) * (tt // SUBLANES) + row // SUBLANES
    src = jnp.stack([step, tt - 1 - step])
    perm = (src[:, :, None] == row[None, None, :]).astype(BF16)
    return perm, perm.transpose(0, 2, 1)


def _rnn(xr3, p):
    bsz, seq, _ = xr3.shape
    tt = TT_RNN
    nt = seq // tt
    hpt = tt // BF16_ROWS
    n_halo = seq // BF16_ROWS
    fwd = lambda b, j: (b, j, 0)
    fwd_p = lambda b, j: (b, jnp.maximum(j * hpt - 1, 0), 0)
    fwd_n = lambda b, j: (b, jnp.minimum((j + 1) * hpt, n_halo - 1), 0)
    bwd = lambda b, j: (b, nt - 1 - j, 0)
    bwd_p = lambda b, j: (b, jnp.maximum((nt - 1 - j) * hpt - 1, 0), 0)
    bwd_n = lambda b, j: (b, jnp.minimum((nt - j) * hpt, n_halo - 1), 0)
    perm, inv = _scan_permutations(tt)
    consts = [perm, inv, p['rnn_conv_w'], p['rnn_conv_b'], p['w_lru'], p['lru_b_a'], p['lru_b_x'], p['lru_lambda']]
    main = (1, tt, D_RNN)
    hal = (1, BF16_ROWS, D_RNN)
    in_specs = [pl.BlockSpec(main, fwd), pl.BlockSpec(hal, fwd_p), pl.BlockSpec(hal, fwd_n),
                pl.BlockSpec(main, bwd), pl.BlockSpec(hal, bwd_p), pl.BlockSpec(hal, bwd_n)]
    in_specs += [_const_spec(c.shape) for c in consts]
    out_shape = [jax.ShapeDtypeStruct((bsz, seq, D_RNN), BF16)] * 2
    out_specs = [pl.BlockSpec(main, fwd), pl.BlockSpec(main, bwd)]
    scratch = [pltpu.VMEM((2, tt // SUBLANES + 4, SUBLANES, D_RNN), F32), pltpu.VMEM((2, tt, D_RNN), F32),
               pltpu.VMEM((2, tt, D_RNN), F32), pltpu.VMEM((2, SUBLANES, D_RNN), F32)]
    return pl.pallas_call(
        functools.partial(_rnn_kernel, n_tiles=nt), grid=(bsz, nt), in_specs=in_specs, out_specs=out_specs,
        out_shape=out_shape, scratch_shapes=scratch,
        compiler_params=pltpu.CompilerParams(dimension_semantics=("parallel", "arbitrary"),
                                             vmem_limit_bytes=VMEM_LIMIT),
        name="rnn",
    )(xr3, xr3, xr3, xr3, xr3, xr3, *consts)


def _attn_kernel(q_ref, k_ref, vt_ref, o_ref, s_ref, p_ref, acc_ref, *, n_kv):
    tk = TK_ATTN
    qt = jnp.transpose(q_ref[0].astype(F32)).astype(BF16)

    def scores(i, slot):
        r0 = pl.multiple_of(i * tk, tk)
        s = _dot(k_ref[0, pl.ds(r0, tk), :], qt)
        s_ref[slot] = s
        return jnp.max(s, axis=0, keepdims=True)

    def exponentials(slot, m, l, cmax):
        m_new = jnp.maximum(m, cmax)
        alpha = jnp.exp2(m - m_new)
        m_rows = jnp.broadcast_to(m_new, (ATTN_ROWS, tq))
        part = jnp.zeros((SUBLANES, tq), F32)
        for r in range(0, tk, ATTN_ROWS):
            pr = jnp.exp2(s_ref[slot, r:r + ATTN_ROWS, :] - m_rows)
            p_ref[slot, r:r + ATTN_ROWS, :] = pr.astype(BF16)
            part = part + jnp.sum(pr.reshape(ATTN_ROWS // SUBLANES, SUBLANES, tq), axis=0)
        return m_new, alpha * l + jnp.sum(part, axis=0, keepdims=True), alpha

    def values(i, slot, alpha):
        acc_ref[...] = alpha * acc_ref[...] + _dot(vt_ref[0, i], p_ref[slot])

    tq = qt.shape[1]
    m = jnp.full((1, tq), -jnp.inf, F32)
    l = jnp.zeros((1, tq), F32)
    acc_ref[...] = jnp.zeros_like(acc_ref)
    cmax = scores(0, 0)
    cmax_next = scores(1, 1)
    m, l, alpha = exponentials(0, m, l, cmax)
    cmax = cmax_next

    alpha_prev = alpha
    for i in range(1, n_kv - 1):
        cur = i % 2
        cmax_next = scores(i + 1, 1 - cur)
        m, l, alpha = exponentials(cur, m, l, cmax)
        values(i - 1, 1 - cur, alpha_prev)
        cmax, alpha_prev = cmax_next, alpha
    last = n_kv - 1
    m, l, alpha = exponentials(last % 2, m, l, cmax)
    values(last - 1, 1 - last % 2, alpha_prev)
    values(last, last % 2, alpha)
    o_ref[...] = jnp.transpose(acc_ref[...] / l).astype(BF16)


def _attn(q, k, vt, bsz, seq):
    n = bsz * seq
    tq = TQ_ATTN
    nq = seq // tq
    n_kv = seq // TK_ATTN
    in_specs = [pl.BlockSpec((1, tq, QK_DIM), lambda b, h, i: (h, b * nq + i, 0)),
                pl.BlockSpec((1, seq, QK_DIM), lambda b, h, i: (h, b, 0)),
                pl.BlockSpec((1, n_kv, V_HEAD, TK_ATTN), lambda b, h, i: (h, b, 0, 0))]
    out_specs = pl.BlockSpec((tq, V_HEAD), lambda b, h, i: (b * nq + i, h))
    assert n_kv % 2 == 0 and n_kv >= 4
    scratch = [pltpu.VMEM((2, TK_ATTN, tq), F32), pltpu.VMEM((2, TK_ATTN, tq), BF16),
               pltpu.VMEM((V_HEAD, tq), F32)]
    return pl.pallas_call(
        functools.partial(_attn_kernel, n_kv=n_kv), grid=(bsz, N_HEADS, nq), in_specs=in_specs,
        out_specs=out_specs, out_shape=jax.ShapeDtypeStruct((n, N_HEADS * V_HEAD), BF16),
        scratch_shapes=scratch,
        compiler_params=pltpu.CompilerParams(dimension_semantics=("parallel", "parallel", "arbitrary"),
                                             vmem_limit_bytes=VMEM_LIMIT),
        name="attn",
    )(q, k, vt)


def _mix_kernel(x_ref, hf_ref, hb_ref, gg_ref, o_ref, gate_ref, lng_ref, lnb_ref, w_or_ref, w_oa_ref, w_out_ref,
                l1g_ref, l1b_ref, h1_ref):
    h = _layer_norm(x_ref[...], lng_ref[...], lnb_ref[...])
    rnn_in = (hf_ref[...].astype(F32) + hb_ref[...].astype(F32)) * gg_ref[...].astype(F32)
    y_rnn = _dot(rnn_in.astype(BF16), w_or_ref[...])
    y_attn = _dot(o_ref[...], w_oa_ref[...])
    gates = gate_ref[...].astype(F32)
    merged = gates[:, :D_MODEL] * y_rnn + gates[:, D_MODEL:] * y_attn
    mix = _dot(merged.astype(BF16), w_out_ref[...])
    h1_ref[...] = _layer_norm(ALPHA * h + mix, l1g_ref[...], l1b_ref[...])


def _mix(x2, hf, hb, gg, o, gate, p):
    n = x2.shape[0]
    tm = TM_MIX
    row = lambda i: (i, 0)
    acts = [x2, hf, hb, gg, o, gate]
    consts = [p['ln_in_g'], p['ln_in_b'], p['w_o_rnn'], p['w_o_attn'], p['w_out'], p['ln1_g'], p['ln1_b']]
    in_specs = [pl.BlockSpec((tm, a.shape[1]), row) for a in acts] + [_const_spec(c.shape) for c in consts]
    return pl.pallas_call(
        _mix_kernel, grid=(n // tm,), in_specs=in_specs, out_specs=pl.BlockSpec((tm, D_MODEL), row),
        out_shape=jax.ShapeDtypeStruct((n, D_MODEL), F32),
        compiler_params=pltpu.CompilerParams(dimension_semantics=("parallel",), vmem_limit_bytes=VMEM_LIMIT),
        name="mix",
    )(*acts, *consts)


def _ffn_kernel(h_ref, hp_ref, hn_ref, wg_ref, wu_ref, cw_ref, cb_ref, wo_ref, l2g_ref, l2b_ref, out_ref,
                ext_ref, g_ref, u_ref, a_ref, *, n_tiles):
    j = pl.program_id(1)
    tm = TM_FFN
    halo = SUBLANES
    hm = h_ref[0]
    ext_ref[0:halo, :] = jnp.where(j > 0, hp_ref[0], 0.0)
    ext_ref[halo:halo + tm, :] = hm
    ext_ref[halo + tm:2 * halo + tm, :] = jnp.where(j < n_tiles - 1, hn_ref[0], 0.0)
    ext = ext_ref[...].astype(BF16)
    hmb = hm.astype(BF16)

    n_chunks = D_FF // FF_CHUNK

    def up_project(c):
        g_ref[c % 2] = _dot(ext, wg_ref[c])
        u_ref[c % 2] = _dot(hmb, wu_ref[c])

    def activate(c):
        g = g_ref.at[c % 2]
        cw = cw_ref[c]
        gc = cb_ref[c] + cw[0:1, :] * g[halo - 1:halo - 1 + tm, :]
        gc = gc + cw[1:2, :] * g[halo:halo + tm, :]
        gc = gc + cw[2:3, :] * g[halo + 1:halo + 1 + tm, :]
        a_ref[c % 2] = (jax.nn.gelu(gc, approximate=True) * u_ref[c % 2]).astype(BF16)

    def down_project(c, acc):
        part = _dot(a_ref[c % 2], wo_ref[c])
        return part if acc is None else acc + part

    acc = None
    up_project(0)
    for c in range(n_chunks):
        if c + 1 < n_chunks:
            up_project(c + 1)
        activate(c)
        if c >= 1:
            acc = down_project(c - 1, acc)
    acc = down_project(n_chunks - 1, acc)
    out_ref[0] = _layer_norm(ALPHA * hm + acc, l2g_ref[...], l2b_ref[...])


def _ffn(h1, p):
    bsz, seq, _ = h1.shape
    tm = TM_FFN
    nt = seq // tm
    hpt = tm // SUBLANES
    n_halo = seq // SUBLANES
    main = (1, tm, D_MODEL)
    hal = (1, SUBLANES, D_MODEL)
    cur = lambda b, j: (b, j, 0)
    prv = lambda b, j: (b, jnp.maximum(j * hpt - 1, 0), 0)
    nxt = lambda b, j: (b, jnp.minimum((j + 1) * hpt, n_halo - 1), 0)
    consts = [p['ffn_w_gate'], p['ffn_w_up'], p['ffn_conv_w'], p['ffn_conv_b'], p['ffn_w_out'], p['ln2_g'],
              p['ln2_b']]
    in_specs = [pl.BlockSpec(main, cur), pl.BlockSpec(hal, prv), pl.BlockSpec(hal, nxt)]
    in_specs += [_const_spec(c.shape) for c in consts]
    scratch = [pltpu.VMEM((tm + 2 * SUBLANES, D_MODEL), F32), pltpu.VMEM((2, tm + 2 * SUBLANES, FF_CHUNK), F32),
               pltpu.VMEM((2, tm, FF_CHUNK), F32), pltpu.VMEM((2, tm, FF_CHUNK), BF16)]
    return pl.pallas_call(
        functools.partial(_ffn_kernel, n_tiles=nt), grid=(bsz, nt), in_specs=in_specs,
        out_specs=pl.BlockSpec(main, cur), out_shape=jax.ShapeDtypeStruct((bsz, seq, D_MODEL), F32),
        scratch_shapes=scratch,
        compiler_params=pltpu.CompilerParams(dimension_semantics=("parallel", "parallel"),
                                             vmem_limit_bytes=VMEM_LIMIT),
        name="ffn",
    )(h1, h1, h1, *consts)


def _rot_cols(w):
    half = QK_ROPE // 2
    return jnp.concatenate([-w[..., half:], w[..., :half]], axis=-1)


def _block_diag_groups(w):
    per = RNN_GROUP // RNN_BLOCK_W
    w4 = w.reshape(D_RNN // RNN_GROUP, per, RNN_BLOCK_W, RNN_BLOCK_W)
    eye = jnp.eye(per, dtype=w.dtype)
    return jnp.einsum('gkij,kl->gkilj', w4, eye).reshape(D_RNN // RNN_GROUP, RNN_GROUP, RNN_GROUP)


def _prepare_params(ln_in_g, ln_in_b, w_in, rnn_conv_w, rnn_conv_b, lru_w_a, lru_b_a, lru_w_x, lru_b_x, lru_lambda,
                    w_o_rnn, q_norm_g, w_uq, kv_norm_g, w_ukv, w_o_attn, b_gate, w_out, ln1_g, ln1_b, ffn_w_in,
                    ffn_conv_w, ffn_conv_b, ffn_w_out, ln2_g, ln2_b):
    l = 0
    row = lambda v: v.reshape(1, -1).astype(F32)
    w = w_in[l]
    c0, c1, c2, c3 = 2 * D_RNN, 2 * D_RNN + Q_LORA, 2 * D_RNN + Q_LORA + KV_LORA, 2 * D_RNN + Q_LORA + KV_LORA + QK_ROPE
    w_kr = w[:, c2:c3]
    uq = w_uq[l].reshape(Q_LORA, N_HEADS, QK_DIM)
    uq_nope = uq[:, :, :QK_NOPE].reshape(Q_LORA, N_HEADS * QK_NOPE)
    uq_rope = uq[:, :, QK_NOPE:]
    ukv = w_ukv[l].reshape(KV_LORA, N_HEADS, QK_NOPE + V_HEAD)
    n_chunks = D_FF // FF_CHUNK
    fw = ffn_w_in[l]
    chunk_cols = lambda m: m.reshape(m.shape[0], n_chunks, FF_CHUNK).transpose(1, 0, 2)
    return {
        'ln_in_g': row(ln_in_g), 'ln_in_b': row(ln_in_b),
        'w_rnn': w[:, :c0].astype(BF16),
        'w_lat': jnp.concatenate([w[:, c0:c2], w_kr, _rot_cols(w_kr)], axis=1).astype(BF16),
        'w_gate': w[:, c3:].astype(BF16),
        'b_gate': row(b_gate[l]),
        'q_norm_g': row(q_norm_g[l]), 'kv_norm_g': row(kv_norm_g[l]),
        'w_uq': jnp.concatenate([uq_nope, uq_rope.reshape(Q_LORA, -1), _rot_cols(uq_rope).reshape(Q_LORA, -1)],
                                axis=1).astype(BF16),
        'w_uk': ukv[:, :, :QK_NOPE].reshape(KV_LORA, N_HEADS * QK_NOPE).astype(BF16),
        'w_uvt': ukv[:, :, QK_NOPE:].reshape(KV_LORA, N_HEADS * V_HEAD).T.astype(BF16),
        'rnn_conv_w': rnn_conv_w[l].astype(F32), 'rnn_conv_b': row(rnn_conv_b[l]),
        'w_lru': (jnp.stack([jnp.concatenate([_block_diag_groups(lru_w_a[l, d]), _block_diag_groups(lru_w_x[l, d])],
                                             axis=2) for d in range(2)]) * 0.5).astype(BF16),
        'lru_b_a': 0.5 * lru_b_a[l].astype(F32), 'lru_b_x': 0.5 * lru_b_x[l].astype(F32),
        'lru_lambda': lru_lambda[l].astype(F32),
        'w_o_rnn': w_o_rnn[l].astype(BF16), 'w_o_attn': w_o_attn[l].astype(BF16), 'w_out': w_out[l].astype(BF16),
        'ln1_g': row(ln1_g[l]), 'ln1_b': row(ln1_b[l]),
        'ffn_w_gate': chunk_cols(fw[:, :D_FF]).astype(BF16),
        'ffn_w_up': chunk_cols(fw[:, D_FF:]).astype(BF16),
        'ffn_conv_w': chunk_cols(ffn_conv_w[l]).astype(F32),
        'ffn_conv_b': ffn_conv_b[l].reshape(n_chunks, 1, FF_CHUNK).astype(F32),
        'ffn_w_out': ffn_w_out[l].reshape(n_chunks, FF_CHUNK, D_MODEL).astype(BF16),
        'ln2_g': row(ln2_g[l]), 'ln2_b': row(ln2_b[l]),
    }


def _rope_tables(seq):
    pos = jnp.arange(seq, dtype=F32)
    inv = ROPE_THETA ** (-jnp.arange(0, QK_ROPE, 2, dtype=F32) / QK_ROPE)
    ang = pos[:, None] * inv[None, :]
    reps = LANES // (QK_ROPE // 2)
    return jnp.tile(jnp.cos(ang), (1, reps)), jnp.tile(jnp.sin(ang), (1, reps))


def _trunk(x, p):
    bsz, seq, _ = x.shape
    n = bsz * seq
    x2 = x.reshape(n, D_MODEL)
    cos2, sin2 = _rope_tables(seq)
    pp = dict(p, cos2=cos2, sin2=sin2)
    xr, gg, gate, q, k, v = _proj(x2, seq, pp)
    hf, hb = _rnn(xr.reshape(bsz, seq, D_RNN), p)
    o = _attn(q, k, v, bsz, seq)
    h1 = _mix(x2, hf.reshape(n, D_RNN), hb.reshape(n, D_RNN), gg, o, gate, p)
    return _ffn(h1.reshape(bsz, seq, D_MODEL), p)


def kernel(x_prompt, x_sample, ln_in_g, ln_in_b, w_in, rnn_conv_w, rnn_conv_b, lru_w_a, lru_b_a, lru_w_x, lru_b_x,
           lru_lambda, w_o_rnn, q_norm_g, w_uq, kv_norm_g, w_ukv, w_o_attn, b_gate, w_out, ln1_g, ln1_b, ffn_w_in,
           ffn_conv_w, ffn_conv_b, ffn_w_out, ln2_g, ln2_b):
    p = _prepare_params(ln_in_g, ln_in_b, w_in, rnn_conv_w, rnn_conv_b, lru_w_a, lru_b_a, lru_w_x, lru_b_x,
                        lru_lambda, w_o_rnn, q_norm_g, w_uq, kv_norm_g, w_ukv, w_o_attn, b_gate, w_out, ln1_g,
                        ln1_b, ffn_w_in, ffn_conv_w, ffn_conv_b, ffn_w_out, ln2_g, ln2_b)
    return _trunk(x_prompt, p), _trunk(x_sample, p)
```

```python
import functools

import jax
import jax.numpy as jnp
from jax import lax
from jax.experimental import pallas as pl
from jax.experimental.pallas import tpu as pltpu

D_MODEL = 1024
D_RNN = 1024
RNN_BLOCKS = 16
RNN_BLOCK_W = D_RNN // RNN_BLOCKS
LRU_C = 8.0
N_HEADS = 8
Q_LORA = 384
KV_LORA = 256
QK_NOPE = 128
QK_ROPE = 64
QK_DIM = QK_NOPE + QK_ROPE
V_HEAD = 128
ROPE_THETA = 10000.0
D_FF = 3072
LN_EPS = 1e-5
RMS_EPS = 1e-6
DEPTH = 1
ALPHA = (2.0 * DEPTH) ** 0.25
LOG2_E = 1.4426950408889634
SQRT_FLOOR = 1e-30

SUBLANES = 8
LANES = 128
BF16_ROWS = 16
VMEM_LIMIT = 56 * 1024 * 1024

TM_PROJ = 512
TT_RNN = 512
RNN_GROUP = 256
TQ_ATTN = 2048
ATTN_SUBTILES = 2
TK_ATTN = TM_PROJ
ATTN_ROWS = 32
TM_MIX = 512
TM_FFN = 512
FF_CHUNK = 1024

BF16 = jnp.bfloat16
F32 = jnp.float32


def _layer_norm(x, g, b):
    mu = jnp.mean(x, axis=-1, keepdims=True)
    xc = x - mu
    var = jnp.mean(xc * xc, axis=-1, keepdims=True)
    return xc * lax.rsqrt(var + LN_EPS) * g + b


def _rms_norm(x, g):
    return x * lax.rsqrt(jnp.mean(x * x, axis=-1, keepdims=True) + RMS_EPS) * g


def _sigmoid(x):
    return 0.5 * jnp.tanh(0.5 * x) + 0.5


def _dot(a, b):
    return jnp.dot(a, b, preferred_element_type=F32)


def _const_spec(shape):
    nd = len(shape)
    return pl.BlockSpec(shape, lambda *_: (0,) * nd, pipeline_mode=pl.Buffered(1))


def _proj_kernel(x_ref, lng_ref, lnb_ref, w_rnn_ref, w_gate_ref, bgate_ref, w_lat_ref, qg_ref, w_uq_ref,
                 kvg_ref, w_uk_ref, w_uvt_ref, cos_ref, sin_ref,
                 xr_ref, gg_ref, gate_ref, q_ref, k_ref, vt_ref):
    h = _layer_norm(x_ref[...], lng_ref[...], lnb_ref[...]).astype(BF16)

    rnn = _dot(h, w_rnn_ref[...])
    xr_ref[...] = rnn[:, :D_RNN].astype(BF16)
    gg_ref[...] = jax.nn.gelu(rnn[:, D_RNN:], approximate=True).astype(BF16)

    gate_ref[...] = _sigmoid(_dot(h, w_gate_ref[...]) + bgate_ref[...]).astype(BF16)

    cos2 = cos_ref[...]
    sin2 = sin_ref[...]
    cos_q = jnp.concatenate([cos2] * (N_HEADS * QK_ROPE // LANES), axis=1)
    sin_q = jnp.concatenate([sin2] * (N_HEADS * QK_ROPE // LANES), axis=1)

    lat = _dot(h, w_lat_ref[...])
    qn = _rms_norm(lat[:, :Q_LORA], qg_ref[...]).astype(BF16)
    qf = _dot(qn, w_uq_ref[...])
    n_nope = N_HEADS * QK_NOPE
    n_rope = N_HEADS * QK_ROPE
    scale = QK_DIM ** -0.5 * LOG2_E
    q_nope = qf[:, :n_nope] * scale
    q_rope = (qf[:, n_nope:n_nope + n_rope] * cos_q + qf[:, n_nope + n_rope:] * sin_q) * scale

    kvn = _rms_norm(lat[:, Q_LORA:Q_LORA + KV_LORA], kvg_ref[...])
    k_nope = _dot(kvn.astype(BF16), w_uk_ref[...])
    vt = _dot(w_uvt_ref[...], jnp.transpose(kvn).astype(BF16))
    kr = lat[:, Q_LORA + KV_LORA:]
    k_rope = (kr[:, :QK_ROPE] * cos2[:, :QK_ROPE] + kr[:, QK_ROPE:] * sin2[:, :QK_ROPE]).astype(BF16)

    for hd in range(N_HEADS):
        q_ref[hd] = jnp.concatenate(
            [q_nope[:, hd * QK_NOPE:(hd + 1) * QK_NOPE], q_rope[:, hd * QK_ROPE:(hd + 1) * QK_ROPE]],
            axis=1).astype(BF16)
        k_ref[hd] = jnp.concatenate([k_nope[:, hd * QK_NOPE:(hd + 1) * QK_NOPE].astype(BF16), k_rope], axis=1)
        vt_ref[hd, 0] = vt[hd * V_HEAD:(hd + 1) * V_HEAD, :].astype(BF16)


def _proj(x2, seq, p):
    n = x2.shape[0]
    tm = TM_PROJ
    nt_seq = seq // tm
    row = lambda i: (i, 0)
    pos = lambda i: (i % nt_seq, 0)
    head_row = lambda i: (0, i, 0)
    consts = [p['ln_in_g'], p['ln_in_b'], p['w_rnn'], p['w_gate'], p['b_gate'], p['w_lat'], p['q_norm_g'],
              p['w_uq'], p['kv_norm_g'], p['w_uk'], p['w_uvt']]
    in_specs = ([pl.BlockSpec((tm, D_MODEL), row)] + [_const_spec(c.shape) for c in consts]
                + [pl.BlockSpec((tm, LANES), pos), pl.BlockSpec((tm, LANES), pos)])
    out_shape = [
        jax.ShapeDtypeStruct((n, D_RNN), BF16),
        jax.ShapeDtypeStruct((n, D_RNN), BF16),
        jax.ShapeDtypeStruct((n, 2 * D_MODEL), BF16),
        jax.ShapeDtypeStruct((N_HEADS, n, QK_DIM), BF16),
        jax.ShapeDtypeStruct((N_HEADS, n, QK_DIM), BF16),
        jax.ShapeDtypeStruct((N_HEADS, n // tm, V_HEAD, tm), BF16),
    ]
    out_specs = [
        pl.BlockSpec((tm, D_RNN), row), pl.BlockSpec((tm, D_RNN), row), pl.BlockSpec((tm, 2 * D_MODEL), row),
        pl.BlockSpec((N_HEADS, tm, QK_DIM), head_row), pl.BlockSpec((N_HEADS, tm, QK_DIM), head_row),
        pl.BlockSpec((N_HEADS, 1, V_HEAD, tm), lambda i: (0, i, 0, 0)),
    ]
    return pl.pallas_call(
        _proj_kernel, grid=(n // tm,), in_specs=in_specs, out_specs=out_specs, out_shape=out_shape,
        compiler_params=pltpu.CompilerParams(dimension_semantics=("parallel",), vmem_limit_bytes=VMEM_LIMIT),
        name="proj",
    )(x2, *consts, p['cos2'], p['sin2'])


def _rnn_kernel(xf_ref, xfp_ref, xfn_ref, xb_ref, xbp_ref, xbn_ref, perm_ref, inv_ref, cw_ref, cb_ref, wg_ref,
                ba_ref, bx_ref, lam_ref, hf_ref, hb_ref, xs_ref, a_ref, b_ref, carry_ref, *, n_tiles):
    j = pl.program_id(1)
    tt = TT_RNN
    n_l = tt // SUBLANES
    top = SUBLANES - 1

    @pl.when(j == 0)
    def _():
        carry_ref[...] = jnp.zeros_like(carry_ref)

    sub = lax.broadcasted_iota(jnp.int32, (SUBLANES, D_RNN), 0)

    def halo_row(blk, r, valid):
        row = jnp.broadcast_to(blk[r:r + 1, :], (SUBLANES, D_RNN))
        return jnp.where(valid, row, 0.0)

    def coefficients(d, x_ref, xp_ref, xn_ref, tile):
        xs = xs_ref.at[d]
        prev = xp_ref[0].astype(F32)
        nxt = xn_ref[0].astype(F32)
        has_prev = tile > 0
        has_next = tile < n_tiles - 1
        xs[2:2 + n_l] = _dot(perm_ref[d], x_ref[0]).reshape(n_l, SUBLANES, D_RNN)

        def before(l, fill):
            return jnp.where(sub >= 1, pltpu.roll(xs[2 + l], 1, 0), fill)

        def after(l, fill):
            return jnp.where(sub < top, pltpu.roll(xs[2 + l], top, 0), fill)

        if d == 0:
            xs[1] = before(n_l - 1, halo_row(prev, BF16_ROWS - 1, has_prev))
            xs[0] = before(n_l - 2, halo_row(prev, BF16_ROWS - 2, has_prev))
            xs[2 + n_l] = after(0, halo_row(nxt, 0, has_next))
            offsets = (-2, -1, 0, 1)
        else:
            xs[1] = before(n_l - 1, halo_row(nxt, 0, has_next))
            xs[2 + n_l] = after(0, halo_row(prev, BF16_ROWS - 1, has_prev))
            xs[3 + n_l] = after(1, halo_row(prev, BF16_ROWS - 2, has_prev))
            offsets = (2, 1, 0, -1)
        u = cb_ref[...] + cw_ref[0:1, :] * xs[2 + offsets[0]:2 + offsets[0] + n_l]
        for k in range(1, 4):
            u = u + cw_ref[k:k + 1, :] * xs[2 + offsets[k]:2 + offsets[k] + n_l]
        u = u.reshape(tt, D_RNN)

        sp = jax.nn.softplus(-lam_ref[d:d + 1, :])
        for g in range(D_RNN // RNN_GROUP):
            cs = slice(g * RNN_GROUP, (g + 1) * RNN_GROUP)
            ug = u[:, cs]
            z = _dot(ug.astype(BF16), wg_ref[d, g])
            tr = jnp.tanh(z[:, :RNN_GROUP] + ba_ref[d:d + 1, cs])
            i = 0.5 * jnp.tanh(z[:, RNN_GROUP:] + bx_ref[d:d + 1, cs]) + 0.5
            half_c = (-0.5 * LRU_C) * sp[:, cs]
            log_a = half_c * tr + half_c
            a = jnp.exp(log_a)
            a_ref[d, :, cs] = a
            y = 1.0 - a * a
            b_ref[d, :, cs] = y * lax.rsqrt(jnp.maximum(y, SQRT_FLOOR)) * (i * ug)

    def body(l, carry):
        r0 = pl.multiple_of(l * SUBLANES, SUBLANES)
        out = []
        for d in range(2):
            hv, pv = carry[d]
            av = a_ref[d, pl.ds(r0, SUBLANES), :]
            hv = av * hv + b_ref[d, pl.ds(r0, SUBLANES), :]
            pv = av * pv
            b_ref[d, pl.ds(r0, SUBLANES), :] = hv
            a_ref[d, pl.ds(r0, SUBLANES), :] = pv
            out.append((hv, pv))
        return tuple(out)

    def finish(d, bv, av, out_ref):
        for s in (1, 2, 4):
            m = sub >= s
            ar = jnp.where(m, pltpu.roll(av, s, 0), 1.0)
            br = jnp.where(m, pltpu.roll(bv, s, 0), 0.0)
            bv = av * br + bv
            av = av * ar
        entry = carry_ref[d]
        ends = av * entry + bv
        chunk_entry = jnp.where(sub >= 1, pltpu.roll(ends, 1, 0), entry)
        carry_ref[d] = jnp.broadcast_to(ends[top:top + 1, :], (SUBLANES, D_RNN))
        hs = (b_ref[d].reshape(n_l, SUBLANES, D_RNN)
              + a_ref[d].reshape(n_l, SUBLANES, D_RNN) * chunk_entry).reshape(tt, D_RNN)
        out_ref[0] = _dot(inv_ref[d], hs.astype(BF16)).astype(BF16)

    coefficients(0, xf_ref, xfp_ref, xfn_ref, j)
    coefficients(1, xb_ref, xbp_ref, xbn_ref, n_tiles - 1 - j)
    zeros = jnp.zeros((SUBLANES, D_RNN), F32)
    start = (zeros, zeros + 1.0)
    (bv0, av0), (bv1, av1) = lax.fori_loop(0, n_l, body, (start, start), unroll=4)
    finish(0, bv0, av0, hf_ref)
    finish(1, bv1, av1, hb_ref)


def _scan_permutations(tt):
    row = jnp.arange(tt)
    step = (row % SUBLANES) * (tt // SUBLANES) + row // SUBLANES
    src = jnp.stack([step, tt - 1 - step])
    perm = (src[:, :, None] == row[None, None, :]).astype(BF16)
    return perm, perm.transpose(0, 2, 1)


def _rnn(xr3, p):
    bsz, seq, _ = xr3.shape
    tt = TT_RNN
    nt = seq // tt
    hpt = tt // BF16_ROWS
    n_halo = seq // BF16_ROWS
    fwd = lambda b, j: (b, j, 0)
    fwd_p = lambda b, j: (b, jnp.maximum(j * hpt - 1, 0), 0)
    fwd_n = lambda b, j: (b, jnp.minimum((j + 1) * hpt, n_halo - 1), 0)
    bwd = lambda b, j: (b, nt - 1 - j, 0)
    bwd_p = lambda b, j: (b, jnp.maximum((nt - 1 - j) * hpt - 1, 0), 0)
    bwd_n = lambda b, j: (b, jnp.minimum((nt - j) * hpt, n_halo - 1), 0)
    perm, inv = _scan_permutations(tt)
    consts = [perm, inv, p['rnn_conv_w'], p['rnn_conv_b'], p['w_lru'], p['lru_b_a'], p['lru_b_x'], p['lru_lambda']]
    main = (1, tt, D_RNN)
    hal = (1, BF16_ROWS, D_RNN)
    in_specs = [pl.BlockSpec(main, fwd), pl.BlockSpec(hal, fwd_p), pl.BlockSpec(hal, fwd_n),
                pl.BlockSpec(main, bwd), pl.BlockSpec(hal, bwd_p), pl.BlockSpec(hal, bwd_n)]
    in_specs += [_const_spec(c.shape) for c in consts]
    out_shape = [jax.ShapeDtypeStruct((bsz, seq, D_RNN), BF16)] * 2
    out_specs = [pl.BlockSpec(main, fwd), pl.BlockSpec(main, bwd)]
    scratch = [pltpu.VMEM((2, tt // SUBLANES + 4, SUBLANES, D_RNN), F32), pltpu.VMEM((2, tt, D_RNN), F32),
               pltpu.VMEM((2, tt, D_RNN), F32), pltpu.VMEM((2, SUBLANES, D_RNN), F32)]
    return pl.pallas_call(
        functools.partial(_rnn_kernel, n_tiles=nt), grid=(bsz, nt), in_specs=in_specs, out_specs=out_specs,
        out_shape=out_shape, scratch_shapes=scratch,
        compiler_params=pltpu.CompilerParams(dimension_semantics=("parallel", "arbitrary"),
                                             vmem_limit_bytes=VMEM_LIMIT),
        name="rnn",
    )(xr3, xr3, xr3, xr3, xr3, xr3, *consts)


def _attn_kernel(q_ref, k_ref, vt_ref, o_ref, s_ref, p_ref, acc_ref, *, n_kv):
    tk = TK_ATTN
    tq = q_ref.shape[1] // ATTN_SUBTILES

    for c in range(ATTN_SUBTILES):
        qt = jnp.transpose(q_ref[0, c * tq:(c + 1) * tq, :].astype(F32)).astype(BF16)

        def scores(i, slot):
            r0 = pl.multiple_of(i * tk, tk)
            s = _dot(k_ref[0, pl.ds(r0, tk), :], qt)
            s_ref[2 * c + slot] = s
            return jnp.max(s, axis=0, keepdims=True)

        def exponentials(slot, m, l, cmax):
            m_new = jnp.maximum(m, cmax)
            alpha = jnp.exp2(m - m_new)
            m_rows = jnp.broadcast_to(m_new, (ATTN_ROWS, tq))
            part = jnp.zeros((SUBLANES, tq), F32)
            for r in range(0, tk, ATTN_ROWS):
                pr = jnp.exp2(s_ref[2 * c + slot, r:r + ATTN_ROWS, :] - m_rows)
                p_ref[2 * c + slot, r:r + ATTN_ROWS, :] = pr.astype(BF16)
                part = part + jnp.sum(pr.reshape(ATTN_ROWS // SUBLANES, SUBLANES, tq), axis=0)
            return m_new, alpha * l + jnp.sum(part, axis=0, keepdims=True), alpha

        def values(i, slot, alpha, first=False):
            pv = _dot(vt_ref[0, i], p_ref[2 * c + slot])
            acc_ref[c] = pv if first else alpha * acc_ref[c] + pv

        m = jnp.full((1, tq), -jnp.inf, F32)
        l = jnp.zeros((1, tq), F32)
        cmax = scores(0, 0)
        alpha_prev = None
        for i in range(n_kv):
            cur = i % 2
            cmax_next = scores(i + 1, 1 - cur) if i + 1 < n_kv else None
            m, l, alpha = exponentials(cur, m, l, cmax)
            if i >= 1:
                values(i - 1, 1 - cur, alpha_prev, first=(i == 1))
            cmax, alpha_prev = cmax_next, alpha
        values(n_kv - 1, (n_kv - 1) % 2, alpha_prev)
        o_ref[c * tq:(c + 1) * tq, :] = jnp.transpose(acc_ref[c] / l).astype(BF16)


def _attn(q, k, vt, bsz, seq):
    n = bsz * seq
    tq = TQ_ATTN
    nq = seq // tq
    n_kv = seq // TK_ATTN
    in_specs = [pl.BlockSpec((1, tq, QK_DIM), lambda b, h, i: (h, b * nq + i, 0)),
                pl.BlockSpec((1, seq, QK_DIM), lambda b, h, i: (h, b, 0)),
                pl.BlockSpec((1, n_kv, V_HEAD, TK_ATTN), lambda b, h, i: (h, b, 0, 0))]
    out_specs = pl.BlockSpec((tq, V_HEAD), lambda b, h, i: (b * nq + i, h))
    assert n_kv % 2 == 0 and n_kv >= 4
    tqc = tq // ATTN_SUBTILES
    scratch = [pltpu.VMEM((2 * ATTN_SUBTILES, TK_ATTN, tqc), F32), pltpu.VMEM((2 * ATTN_SUBTILES, TK_ATTN, tqc), BF16),
               pltpu.VMEM((ATTN_SUBTILES, V_HEAD, tqc), F32)]
    return pl.pallas_call(
        functools.partial(_attn_kernel, n_kv=n_kv), grid=(bsz, N_HEADS, nq), in_specs=in_specs,
        out_specs=out_specs, out_shape=jax.ShapeDtypeStruct((n, N_HEADS * V_HEAD), BF16),
        scratch_shapes=scratch,
        compiler_params=pltpu.CompilerParams(dimension_semantics=("parallel", "parallel", "arbitrary"),
                                             vmem_limit_bytes=VMEM_LIMIT),
        name="attn",
    )(q, k, vt)


def _mix_kernel(x_ref, hf_ref, hb_ref, gg_ref, o_ref, gate_ref, lng_ref, lnb_ref, w_or_ref, w_oa_ref, w_out_ref,
                l1g_ref, l1b_ref, h1_ref):
    h = _layer_norm(x_ref[...], lng_ref[...], lnb_ref[...])
    rnn_in = (hf_ref[...].astype(F32) + hb_ref[...].astype(F32)) * gg_ref[...].astype(F32)
    y_rnn = _dot(rnn_in.astype(BF16), w_or_ref[...])
    y_attn = _dot(o_ref[...], w_oa_ref[...])
    gates = gate_ref[...].astype(F32)
    merged = gates[:, :D_MODEL] * y_rnn + gates[:, D_MODEL:] * y_attn
    mix = _dot(merged.astype(BF16), w_out_ref[...])
    h1_ref[...] = _layer_norm(ALPHA * h + mix, l1g_ref[...], l1b_ref[...])


def _mix(x2, hf, hb, gg, o, gate, p):
    n = x2.shape[0]
    tm = TM_MIX
    row = lambda i: (i, 0)
    acts = [x2, hf, hb, gg, o, gate]
    consts = [p['ln_in_g'], p['ln_in_b'], p['w_o_rnn'], p['w_o_attn'], p['w_out'], p['ln1_g'], p['ln1_b']]
    in_specs = [pl.BlockSpec((tm, a.shape[1]), row) for a in acts] + [_const_spec(c.shape) for c in consts]
    return pl.pallas_call(
        _mix_kernel, grid=(n // tm,), in_specs=in_specs, out_specs=pl.BlockSpec((tm, D_MODEL), row),
        out_shape=jax.ShapeDtypeStruct((n, D_MODEL), F32),
        compiler_params=pltpu.CompilerParams(dimension_semantics=("parallel",), vmem_limit_bytes=VMEM_LIMIT),
        name="mix",
    )(*acts, *consts)


def _ffn_kernel(h_ref, hp_ref, hn_ref, wg_ref, wu_ref, cw_ref, cb_ref, wo_ref, l2g_ref, l2b_ref, out_ref,
                ext_ref, g_ref, u_ref, a_ref, *, n_tiles):
    j = pl.program_id(1)
    tm = TM_FFN
    halo = SUBLANES
    hm = h_ref[0]
    ext_ref[0:halo, :] = jnp.where(j > 0, hp_ref[0], 0.0)
    ext_ref[halo:halo + tm, :] = hm
    ext_ref[halo + tm:2 * halo + tm, :] = jnp.where(j < n_tiles - 1, hn_ref[0], 0.0)
    ext = ext_ref[...].astype(BF16)
    hmb = hm.astype(BF16)

    n_chunks = D_FF // FF_CHUNK

    def up_project(c):
        g_ref[c % 2] = _dot(ext, wg_ref[c])
        u_ref[c % 2] = _dot(hmb, wu_ref[c])

    def activate(c):
        g = g_ref.at[c % 2]
        cw = cw_ref[c]
        gc = cb_ref[c] + cw[0:1, :] * g[halo - 1:halo - 1 + tm, :]
        gc = gc + cw[1:2, :] * g[halo:halo + tm, :]
        gc = gc + cw[2:3, :] * g[halo + 1:halo + 1 + tm, :]
        a_ref[c % 2] = (jax.nn.gelu(gc, approximate=True) * u_ref[c % 2]).astype(BF16)

    def down_project(c, acc):
        part = _dot(a_ref[c % 2], wo_ref[c])
        return part if acc is None else acc + part

    acc = None
    up_project(0)
    for c in range(n_chunks):
        if c + 1 < n_chunks:
            up_project(c + 1)
        activate(c)
        if c >= 1:
            acc = down_project(c - 1, acc)
    acc = down_project(n_chunks - 1, acc)
    out_ref[0] = _layer_norm(ALPHA * hm + acc, l2g_ref[...], l2b_ref[...])


def _ffn(h1, p):
    bsz, seq, _ = h1.shape
    tm = TM_FFN
    nt = seq // tm
    hpt = tm // SUBLANES
    n_halo = seq // SUBLANES
    main = (1, tm, D_MODEL)
    hal = (1, SUBLANES, D_MODEL)
    cur = lambda b, j: (b, j, 0)
    prv = lambda b, j: (b, jnp.maximum(j * hpt - 1, 0), 0)
    nxt = lambda b, j: (b, jnp.minimum((j + 1) * hpt, n_halo - 1), 0)
    consts = [p['ffn_w_gate'], p['ffn_w_up'], p['ffn_conv_w'], p['ffn_conv_b'], p['ffn_w_out'], p['ln2_g'],
              p['ln2_b']]
    in_specs = [pl.BlockSpec(main, cur), pl.BlockSpec(hal, prv), pl.BlockSpec(hal, nxt)]
    in_specs += [_const_spec(c.shape) for c in consts]
    scratch = [pltpu.VMEM((tm + 2 * SUBLANES, D_MODEL), F32), pltpu.VMEM((2, tm + 2 * SUBLANES, FF_CHUNK), F32),
               pltpu.VMEM((2, tm, FF_CHUNK), F32), pltpu.VMEM((2, tm, FF_CHUNK), BF16)]
    return pl.pallas_call(
        functools.partial(_ffn_kernel, n_tiles=nt), grid=(bsz, nt), in_specs=in_specs,
        out_specs=pl.BlockSpec(main, cur), out_shape=jax.ShapeDtypeStruct((bsz, seq, D_MODEL), F32),
        scratch_shapes=scratch,
        compiler_params=pltpu.CompilerParams(dimension_semantics=("parallel", "parallel"),
                                             vmem_limit_bytes=VMEM_LIMIT),
        name="ffn",
    )(h1, h1, h1, *consts)


def _rot_cols(w):
    half = QK_ROPE // 2
    return jnp.concatenate([-w[..., half:], w[..., :half]], axis=-1)


def _block_diag_groups(w):
    per = RNN_GROUP // RNN_BLOCK_W
    w4 = w.reshape(D_RNN // RNN_GROUP, per, RNN_BLOCK_W, RNN_BLOCK_W)
    eye = jnp.eye(per, dtype=w.dtype)
    return jnp.einsum('gkij,kl->gkilj', w4, eye).reshape(D_RNN // RNN_GROUP, RNN_GROUP, RNN_GROUP)


def _prepare_params(ln_in_g, ln_in_b, w_in, rnn_conv_w, rnn_conv_b, lru_w_a, lru_b_a, lru_w_x, lru_b_x, lru_lambda,
                    w_o_rnn, q_norm_g, w_uq, kv_norm_g, w_ukv, w_o_attn, b_gate, w_out, ln1_g, ln1_b, ffn_w_in,
                    ffn_conv_w, ffn_conv_b, ffn_w_out, ln2_g, ln2_b):
    l = 0
    row = lambda v: v.reshape(1, -1).astype(F32)
    w = w_in[l]
    c0, c1, c2, c3 = 2 * D_RNN, 2 * D_RNN + Q_LORA, 2 * D_RNN + Q_LORA + KV_LORA, 2 * D_RNN + Q_LORA + KV_LORA + QK_ROPE
    w_kr = w[:, c2:c3]
    uq = w_uq[l].reshape(Q_LORA, N_HEADS, QK_DIM)
    uq_nope = uq[:, :, :QK_NOPE].reshape(Q_LORA, N_HEADS * QK_NOPE)
    uq_rope = uq[:, :, QK_NOPE:]
    ukv = w_ukv[l].reshape(KV_LORA, N_HEADS, QK_NOPE + V_HEAD)
    n_chunks = D_FF // FF_CHUNK
    fw = ffn_w_in[l]
    chunk_cols = lambda m: m.reshape(m.shape[0], n_chunks, FF_CHUNK).transpose(1, 0, 2)
    return {
        'ln_in_g': row(ln_in_g), 'ln_in_b': row(ln_in_b),
        'w_rnn': w[:, :c0].astype(BF16),
        'w_lat': jnp.concatenate([w[:, c0:c2], w_kr, _rot_cols(w_kr)], axis=1).astype(BF16),
        'w_gate': w[:, c3:].astype(BF16),
        'b_gate': row(b_gate[l]),
        'q_norm_g': row(q_norm_g[l]), 'kv_norm_g': row(kv_norm_g[l]),
        'w_uq': jnp.concatenate([uq_nope, uq_rope.reshape(Q_LORA, -1), _rot_cols(uq_rope).reshape(Q_LORA, -1)],
                                axis=1).astype(BF16),
        'w_uk': ukv[:, :, :QK_NOPE].reshape(KV_LORA, N_HEADS * QK_NOPE).astype(BF16),
        'w_uvt': ukv[:, :, QK_NOPE:].reshape(KV_LORA, N_HEADS * V_HEAD).T.astype(BF16),
        'rnn_conv_w': rnn_conv_w[l].astype(F32), 'rnn_conv_b': row(rnn_conv_b[l]),
        'w_lru': (jnp.stack([jnp.concatenate([_block_diag_groups(lru_w_a[l, d]), _block_diag_groups(lru_w_x[l, d])],
                                             axis=2) for d in range(2)]) * 0.5).astype(BF16),
        'lru_b_a': 0.5 * lru_b_a[l].astype(F32), 'lru_b_x': 0.5 * lru_b_x[l].astype(F32),
        'lru_lambda': lru_lambda[l].astype(F32),
        'w_o_rnn': w_o_rnn[l].astype(BF16), 'w_o_attn': w_o_attn[l].astype(BF16), 'w_out': w_out[l].astype(BF16),
        'ln1_g': row(ln1_g[l]), 'ln1_b': row(ln1_b[l]),
        'ffn_w_gate': chunk_cols(fw[:, :D_FF]).astype(BF16),
        'ffn_w_up': chunk_cols(fw[:, D_FF:]).astype(BF16),
        'ffn_conv_w': chunk_cols(ffn_conv_w[l]).astype(F32),
        'ffn_conv_b': ffn_conv_b[l].reshape(n_chunks, 1, FF_CHUNK).astype(F32),
        'ffn_w_out': ffn_w_out[l].reshape(n_chunks, FF_CHUNK, D_MODEL).astype(BF16),
        'ln2_g': row(ln2_g[l]), 'ln2_b': row(ln2_b[l]),
    }


def _rope_tables(seq):
    pos = jnp.arange(seq, dtype=F32)
    inv = ROPE_THETA ** (-jnp.arange(0, QK_ROPE, 2, dtype=F32) / QK_ROPE)
    ang = pos[:, None] * inv[None, :]
    reps = LANES // (QK_ROPE // 2)
    return jnp.tile(jnp.cos(ang), (1, reps)), jnp.tile(jnp.sin(ang), (1, reps))


def _trunk(x, p):
    bsz, seq, _ = x.shape
    n = bsz * seq
    x2 = x.reshape(n, D_MODEL)
    cos2, sin2 = _rope_tables(seq)
    pp = dict(p, cos2=cos2, sin2=sin2)
    xr, gg, gate, q, k, v = _proj(x2, seq, pp)
    hf, hb = _rnn(xr.reshape(bsz, seq, D_RNN), p)
    o = _attn(q, k, v, bsz, seq)
    h1 = _mix(x2, hf.reshape(n, D_RNN), hb.reshape(n, D_RNN), gg, o, gate, p)
    return _ffn(h1.reshape(bsz, seq, D_MODEL), p)


def kernel(x_prompt, x_sample, ln_in_g, ln_in_b, w_in, rnn_conv_w, rnn_conv_b, lru_w_a, lru_b_a, lru_w_x, lru_b_x,
           lru_lambda, w_o_rnn, q_norm_g, w_uq, kv_norm_g, w_ukv, w_o_attn, b_gate, w_out, ln1_g, ln1_b, ffn_w_in,
           ffn_conv_w, ffn_conv_b, ffn_w_out, ln2_g, ln2_b):
    p = _prepare_params(ln_in_g, ln_in_b, w_in, rnn_conv_w, rnn_conv_b, lru_w_a, lru_b_a, lru_w_x, lru_b_x,
                        lru_lambda, w_o_rnn, q_norm_g, w_uq, kv_norm_g, w_ukv, w_o_attn, b_gate, w_out, ln1_g,
                        ln1_b, ffn_w_in, ffn_conv_w, ffn_conv_b, ffn_w_out, ln2_g, ln2_b)
    return _trunk(x_prompt, p), _trunk(x_sample, p)
```
